```python
import jax, jax.numpy as jnp
from jax import lax
import numpy as np

D_MODEL = 4096
BATCH = 1
SEQ = 16384
DEPTH = 4

CHUNK = 64
N_EVEN = (DEPTH + 1) // 2
N_ODD = DEPTH // 2

MIX_WIDTH = D_MODEL
D_A = MIX_WIDTH // 2
D_B = MIX_WIDTH // 2
A_GROUPS = 16
B_GROUPS = 16
A_CONV_W = 3
B_CONV_W = 31
P_IN = 3 * D_A + 2 * D_B

POOL_WINDOWS = (2, 4, 8, 16)
N_POOL_GROUPS = len(POOL_WINDOWS)
POOL_GROUP = D_MODEL // N_POOL_GROUPS

D_FF = 11008
FFN_CONV_W = 3
EPS = 1e-6

kernel_name = "hybrid_conv_pool_streaming_encoder"


def rms_norm(x, g):
    xf = x.astype(jnp.float32)
    y = xf * lax.rsqrt(jnp.mean(xf * xf, axis=-1, keepdims=True) + EPS)
    return (y * g.astype(jnp.float32)).astype(x.dtype)


def layer_norm(x, g, b):
    xf = x.astype(jnp.float32)
    mu = jnp.mean(xf, axis=-1, keepdims=True)
    xc = xf - mu
    y = xc * lax.rsqrt(jnp.mean(xc * xc, axis=-1, keepdims=True) + EPS)
    return (y * g.astype(jnp.float32) + b.astype(jnp.float32)).astype(x.dtype)


def causal_dwconv(x, w):
    k, c = w.shape
    return lax.conv_general_dilated(
        x, w[:, None, :].astype(x.dtype),
        window_strides=(1,), padding=((k - 1, 0),),
        dimension_numbers=("NWC", "WIO", "NWC"),
        feature_group_count=c)


def hybrid_conv_mixer(h, w_in, conv_a, conv_b, conv_b_bias, ln_g, ln_b, w_out):
    p = h @ w_in
    ax, ac, ab, bv, bg = jnp.split(
        p, [D_A, 2 * D_A, 3 * D_A, 3 * D_A + D_B], axis=-1)
    y_a = ab * causal_dwconv(ac * ax, conv_a)
    u = bv * jax.nn.sigmoid(bg)
    u = causal_dwconv(u, conv_b) + conv_b_bias
    y_b = jax.nn.silu(layer_norm(u, ln_g, ln_b))
    return jnp.concatenate([y_a, y_b], axis=-1) @ w_out


def multiscale_pool_mixer(h, w_groups, scale):
    b, s, d = h.shape
    hg = h.reshape(b, s, N_POOL_GROUPS, POOL_GROUP)
    cs = jnp.cumsum(hg.astype(jnp.float32), axis=1)
    pos = jnp.arange(s)
    diffs = []
    for g, win in enumerate(POOL_WINDOWS):
        c = cs[:, :, g]
        lagged = jnp.pad(c, ((0, 0), (win, 0), (0, 0)))[:, :s]
        cnt = jnp.minimum(pos + 1, win).astype(jnp.float32)[None, :, None]
        diffs.append((c - lagged) / cnt - hg[:, :, g].astype(jnp.float32))
    dpool = jnp.stack(diffs, axis=2).astype(h.dtype)
    y = jnp.einsum("bsgc,gcd->bsgd", dpool, w_groups).reshape(b, s, d)
    return y * scale


def conv_ffn(h, w_up, conv, conv_bias, w_down):
    u = causal_dwconv(h @ w_up, conv) + conv_bias
    gate, up = jnp.split(u, 2, axis=-1)
    return (jax.nn.silu(gate) * up) @ w_down


def setup_inputs(seed: int = 0) -> dict:
    key = jax.random.key(seed)
    ks = jax.random.split(key, 18)
    n = jax.random.normal
    f32 = jnp.float32
    return {
        "x": n(ks[0], (BATCH, SEQ, D_MODEL), f32),
        "mix_norm": 1.0 + 0.05 * n(ks[1], (DEPTH, D_MODEL), f32),
        "ffn_norm": 1.0 + 0.05 * n(ks[2], (DEPTH, D_MODEL), f32),
        "final_norm": 1.0 + 0.05 * n(ks[3], (D_MODEL,), f32),
        "hyb_w_in": n(ks[4], (N_EVEN, D_MODEL, P_IN), f32) * D_MODEL ** -0.5,
        "hyb_conv_a": n(ks[5], (N_EVEN, A_CONV_W, D_A), f32) * A_CONV_W ** -0.5,
        "hyb_conv_b": n(ks[6], (N_EVEN, B_CONV_W, D_B), f32) * B_CONV_W ** -0.5,
        "hyb_conv_b_bias": 0.02 * n(ks[7], (N_EVEN, D_B), f32),
        "hyb_ln_g": 1.0 + 0.05 * n(ks[8], (N_EVEN, D_B), f32),
        "hyb_ln_b": 0.02 * n(ks[9], (N_EVEN, D_B), f32),
        "hyb_w_out": n(ks[10], (N_EVEN, MIX_WIDTH, D_MODEL), f32) * MIX_WIDTH ** -0.5,
        "pool_w": n(ks[11], (N_ODD, N_POOL_GROUPS, POOL_GROUP, POOL_GROUP), f32) * POOL_GROUP ** -0.5,
        "pool_scale": 0.5 + 0.1 * n(ks[12], (N_ODD, D_MODEL), f32),
        "ffn_w_up": n(ks[13], (DEPTH, D_MODEL, 2 * D_FF), f32) * D_MODEL ** -0.5,
        "ffn_conv": n(ks[14], (DEPTH, FFN_CONV_W, 2 * D_FF), f32) * FFN_CONV_W ** -0.5,
        "ffn_conv_bias": 0.02 * n(ks[15], (DEPTH, 2 * D_FF), f32),
        "ffn_w_down": n(ks[16], (DEPTH, D_FF, D_MODEL), f32) * D_FF ** -0.5,
    }


def reference(x, mix_norm, ffn_norm, final_norm, hyb_w_in, hyb_conv_a, hyb_conv_b,
              hyb_conv_b_bias, hyb_ln_g, hyb_ln_b, hyb_w_out, pool_w, pool_scale,
              ffn_w_up, ffn_conv, ffn_conv_bias, ffn_w_down):
    for l in range(DEPTH):
        h = rms_norm(x, mix_norm[l])
        if l % 2 == 0:
            i = l // 2
            y = hybrid_conv_mixer(h, hyb_w_in[i], hyb_conv_a[i], hyb_conv_b[i],
                                  hyb_conv_b_bias[i], hyb_ln_g[i], hyb_ln_b[i],
                                  hyb_w_out[i])
        else:
            i = l // 2
            y = multiscale_pool_mixer(h, pool_w[i], pool_scale[i])
        x = x + y
        h = rms_norm(x, ffn_norm[l])
        x = x + conv_ffn(h, ffn_w_up[l], ffn_conv[l], ffn_conv_bias[l], ffn_w_down[l])
    return rms_norm(x, final_norm)
```

```python
import functools

import jax
import jax.numpy as jnp
from jax import lax
from jax.experimental import pallas as pl
from jax.experimental.pallas import tpu as pltpu

EPS = 1e-6
POOL_WINDOWS = (2, 4, 8, 16)
F32 = jnp.float32
BF16 = jnp.bfloat16

VMEM_LIMIT_BYTES = 56 * 1024 * 1024
SUBLANES = 8
LANES = 128
COL_TILE = 256


def _tiles(m_rows):
    return dict(
        norm=min(256, m_rows),
        ffn_up=min(1024, m_rows),
        down=min(512, m_rows),
        hyb_in=min(512, m_rows),
        convb=min(256, m_rows),
        pool=min(256, m_rows),
    )


def _params(*sem):
    return pltpu.CompilerParams(dimension_semantics=sem, vmem_limit_bytes=VMEM_LIMIT_BYTES)


def _sigmoid(v):
    return 1.0 / (1.0 + jnp.exp(-v))


def _rms(x, g):
    ms = jnp.mean(x * x, axis=-1, keepdims=True)
    return x * lax.rsqrt(ms + EPS) * g


def _rmsnorm_kernel(x_ref, g_ref, o_ref):
    o_ref[...] = _rms(x_ref[...], g_ref[...]).astype(o_ref.dtype)


def _rmsnorm(x, g, out_dtype, tm):
    m, d = x.shape
    return pl.pallas_call(
        _rmsnorm_kernel,
        grid=(m // tm,),
        in_specs=[pl.BlockSpec((tm, d), lambda i: (i, 0)),
                  pl.BlockSpec((1, d), lambda i: (0, 0))],
        out_specs=pl.BlockSpec((tm, d), lambda i: (i, 0)),
        out_shape=jax.ShapeDtypeStruct((m, d), out_dtype),
        compiler_params=_params("parallel"),
        name="rmsnorm",
    )(x, g.reshape(1, d))


def _conv3_rows(u, taps, halo_ref, ext_ref, m, n, tm):
    @pl.when(m == 0)
    def _():
        halo_ref[n] = jnp.zeros(halo_ref.shape[1:], F32)

    ext_ref[0:SUBLANES, :] = halo_ref[n]
    ext_ref[SUBLANES:, :] = u
    halo_ref[n] = u[tm - SUBLANES:, :]
    return (taps[2:3, :] * u
            + taps[1:2, :] * ext_ref[pl.ds(SUBLANES - 1, tm), :]
            + taps[0:1, :] * ext_ref[pl.ds(SUBLANES - 2, tm), :])


def _ffn_up_kernel(h_ref, wg_ref, wu_ref, cg_ref, cu_ref, bg_ref, bu_ref, o_ref,
                   halo_g, halo_u, ext_g, ext_u, *, tm):
    m = pl.program_id(0)
    n = pl.program_id(1)
    h = h_ref[...]
    ug = jnp.dot(h, wg_ref[...], preferred_element_type=F32)
    uu = jnp.dot(h, wu_ref[...], preferred_element_type=F32)
    gate = _conv3_rows(ug, cg_ref[...], halo_g, ext_g, m, n, tm) + bg_ref[...]
    up = _conv3_rows(uu, cu_ref[...], halo_u, ext_u, m, n, tm) + bu_ref[...]
    o_ref[...] = ((gate * _sigmoid(gate)) * up).astype(o_ref.dtype)


def _ffn_up(h, w_up, conv, bias, tm):
    m, d = h.shape
    f = w_up.shape[1] // 2
    tn = COL_TILE
    nt = f // tn
    bias2 = bias.reshape(1, 2 * f)
    kt = conv.shape[0]
    return pl.pallas_call(
        functools.partial(_ffn_up_kernel, tm=tm),
        grid=(m // tm, nt),
        in_specs=[
            pl.BlockSpec((tm, d), lambda i, j: (i, 0)),
            pl.BlockSpec((d, tn), lambda i, j: (0, j)),
            pl.BlockSpec((d, tn), lambda i, j: (0, nt + j)),
            pl.BlockSpec((kt, tn), lambda i, j: (0, j)),
            pl.BlockSpec((kt, tn), lambda i, j: (0, nt + j)),
            pl.BlockSpec((1, tn), lambda i, j: (0, j)),
            pl.BlockSpec((1, tn), lambda i, j: (0, nt + j)),
        ],
        out_specs=pl.BlockSpec((tm, tn), lambda i, j: (i, j)),
        out_shape=jax.ShapeDtypeStruct((m, f), BF16),
        scratch_shapes=[
            pltpu.VMEM((nt, SUBLANES, tn), F32),
            pltpu.VMEM((nt, SUBLANES, tn), F32),
            pltpu.VMEM((tm + SUBLANES, tn), F32),
            pltpu.VMEM((tm + SUBLANES, tn), F32),
        ],
        compiler_params=_params("arbitrary", "arbitrary"),
        name="ffn_up",
    )(h, w_up, w_up, conv, conv, bias2, bias2)


def _mm_res_kernel(a_ref, w_ref, x_ref, o_ref):
    o_ref[...] = x_ref[...] + jnp.dot(a_ref[...], w_ref[...], preferred_element_type=F32)


def _mm_res(a, w, x, tm, tn):
    m, k = a.shape
    n = w.shape[1]
    return pl.pallas_call(
        _mm_res_kernel,
        grid=(m // tm, n // tn),
        in_specs=[pl.BlockSpec((tm, k), lambda i, j: (i, 0)),
                  pl.BlockSpec((k, tn), lambda i, j: (0, j)),
                  pl.BlockSpec((tm, tn), lambda i, j: (i, j))],
        out_specs=pl.BlockSpec((tm, tn), lambda i, j: (i, j)),
        out_shape=jax.ShapeDtypeStruct((m, n), F32),
        compiler_params=_params("parallel", "parallel"),
        name="mm_res",
    )(a, w, x)


def _mm2_res_kernel(a1_ref, a2_ref, w1_ref, w2_ref, x_ref, o_ref):
    acc = jnp.dot(a1_ref[...], w1_ref[...], preferred_element_type=F32)
    acc = acc + jnp.dot(a2_ref[...], w2_ref[...], preferred_element_type=F32)
    o_ref[...] = x_ref[...] + acc


def _mm2_res(a1, a2, w, x, tm, tn):
    m, k1 = a1.shape
    k2 = a2.shape[1]
    assert k1 == k2 and w.shape[0] == k1 + k2
    n = w.shape[1]
    return pl.pallas_call(
        _mm2_res_kernel,
        grid=(m // tm, n // tn),
        in_specs=[pl.BlockSpec((tm, k1), lambda i, j: (i, 0)),
                  pl.BlockSpec((tm, k2), lambda i, j: (i, 0)),
                  pl.BlockSpec((k1, tn), lambda i, j: (0, j)),
                  pl.BlockSpec((k2, tn), lambda i, j: (1, j)),
                  pl.BlockSpec((tm, tn), lambda i, j: (i, j))],
        out_specs=pl.BlockSpec((tm, tn), lambda i, j: (i, j)),
        out_shape=jax.ShapeDtypeStruct((m, n), F32),
        compiler_params=_params("parallel", "parallel"),
        name="mm2_res",
    )(a1, a2, w, w, x)


def _hyb_in_kernel(h_ref, wx_ref, wc_ref, wb_ref, wv_ref, wg_ref, ca_ref,
                   ya_ref, u_ref, halo, ext, *, tm):
    m = pl.program_id(0)
    n = pl.program_id(1)
    h = h_ref[...]
    dot = lambda w_ref: jnp.dot(h, w_ref[...], preferred_element_type=F32)
    z = dot(wc_ref) * dot(wx_ref)
    conv = _conv3_rows(z, ca_ref[...], halo, ext, m, n, tm)
    ya_ref[...] = (dot(wb_ref) * conv).astype(ya_ref.dtype)
    u_ref[...] = dot(wv_ref) * _sigmoid(dot(wg_ref))


def _hyb_in(h, w_in, conv_a, d_a, d_b, tm):
    m, d = h.shape
    assert d_a == d_b and w_in.shape[1] == 3 * d_a + 2 * d_b
    tn = COL_TILE
    nt = d_a // tn
    kt = conv_a.shape[0]
    wspec = lambda off: pl.BlockSpec((d, tn), lambda i, j: (0, off * nt + j))
    return pl.pallas_call(
        functools.partial(_hyb_in_kernel, tm=tm),
        grid=(m // tm, nt),
        in_specs=[pl.BlockSpec((tm, d), lambda i, j: (i, 0)),
                  wspec(0), wspec(1), wspec(2), wspec(3), wspec(4),
                  pl.BlockSpec((kt, tn), lambda i, j: (0, j))],
        out_specs=[pl.BlockSpec((tm, tn), lambda i, j: (i, j)),
                   pl.BlockSpec((tm, tn), lambda i, j: (i, j))],
        out_shape=[jax.ShapeDtypeStruct((m, d_a), BF16),
                   jax.ShapeDtypeStruct((m, d_b), F32)],
        scratch_shapes=[pltpu.VMEM((nt, SUBLANES, tn), F32),
                        pltpu.VMEM((tm + SUBLANES, tn), F32)],
        compiler_params=_params("arbitrary", "arbitrary"),
        name="hyb_in",
    )(h, w_in, w_in, w_in, w_in, w_in, conv_a)


def _convb_kernel(u_ref, uh_ref, cw_ref, cb_ref, g_ref, b_ref, o_ref, ext, cv,
                  *, tm, halo, kw):
    m = pl.program_id(0)
    c = u_ref.shape[1]

    @pl.when(m == 0)
    def _():
        ext[0:halo, :] = jnp.zeros((halo, c), F32)

    @pl.when(m > 0)
    def _():
        ext[0:halo, :] = uh_ref[...]

    ext[halo:, :] = u_ref[...]
    base = halo - (kw - 1)

    def col_block(cb, carry):
        cols = pl.ds(pl.multiple_of(cb * LANES, LANES), LANES)
        acc = jnp.zeros((tm, LANES), F32)
        for j in range(kw):
            acc = acc + cw_ref[j:j + 1, cols] * ext[pl.ds(base + j, tm), cols]
        cv[:, cols] = acc
        return carry

    lax.fori_loop(0, c // LANES, col_block, 0)
    v = cv[...] + cb_ref[...]
    mu = jnp.mean(v, axis=-1, keepdims=True)
    vc = v - mu
    y = vc * lax.rsqrt(jnp.mean(vc * vc, axis=-1, keepdims=True) + EPS)
    y = y * g_ref[...] + b_ref[...]
    o_ref[...] = (y * _sigmoid(y)).astype(o_ref.dtype)


def _convb(u, conv_w, conv_b, ln_g, ln_b, tm):
    m, c = u.shape
    kw = conv_w.shape[0]
    halo = -(-(kw - 1) // SUBLANES) * SUBLANES
    assert tm % halo == 0
    r = tm // halo
    row = lambda v: v.reshape(1, c)
    return pl.pallas_call(
        functools.partial(_convb_kernel, tm=tm, halo=halo, kw=kw),
        grid=(m // tm,),
        in_specs=[pl.BlockSpec((tm, c), lambda i: (i, 0)),
                  pl.BlockSpec((halo, c), lambda i: (jnp.maximum(i * r - 1, 0), 0)),
                  pl.BlockSpec((kw, c), lambda i: (0, 0)),
                  pl.BlockSpec((1, c), lambda i: (0, 0)),
                  pl.BlockSpec((1, c), lambda i: (0, 0)),
                  pl.BlockSpec((1, c), lambda i: (0, 0))],
        out_specs=pl.BlockSpec((tm, c), lambda i: (i, 0)),
        out_shape=jax.ShapeDtypeStruct((m, c), BF16),
        scratch_shapes=[pltpu.VMEM((tm + halo, c), F32),
                        pltpu.VMEM((tm, c), F32)],
        compiler_params=_params("parallel"),
        name="convb_ln",
    )(u, u, conv_w, row(conv_b), row(ln_g), row(ln_b))


def _pool_kernel(x_ref, xh_ref, g_ref, w_ref, s_ref, o_ref, he, *, tm, halo, windows):
    m = pl.program_id(0)
    g = g_ref[...]
    d = x_ref.shape[1]
    gc = d // len(windows)

    @pl.when(m == 0)
    def _():
        he[0:halo, :] = jnp.zeros((halo, d), F32)

    @pl.when(m > 0)
    def _():
        he[0:halo, :] = _rms(xh_ref[...], g)

    he[halo:, :] = _rms(x_ref[...], g)
    pos = m * tm + lax.broadcasted_iota(jnp.int32, (tm, 1), 0)
    for gi, win in enumerate(windows):
        cols = slice(gi * gc, (gi + 1) * gc)
        tok = he[pl.ds(halo, tm), cols]
        s = tok
        for k in range(1, win):
            s = s + he[pl.ds(halo - k, tm), cols]
        cnt = jnp.minimum(pos + 1, win).astype(F32)
        dp = s / cnt - tok
        y = jnp.dot(dp.astype(BF16), w_ref[gi], preferred_element_type=F32)
        o_ref[:, cols] = x_ref[:, cols] + y * s_ref[:, cols]


def _pool_mixer(x, g, w_groups, scale, tm):
    m, d = x.shape
    ng, gc, _ = w_groups.shape
    windows = POOL_WINDOWS
    assert ng == len(windows) and ng * gc == d
    halo = -(-(max(windows) - 1) // SUBLANES) * SUBLANES
    assert tm % halo == 0
    r = tm // halo
    return pl.pallas_call(
        functools.partial(_pool_kernel, tm=tm, halo=halo, windows=windows),
        grid=(m // tm,),
        in_specs=[pl.BlockSpec((tm, d), lambda i: (i, 0)),
                  pl.BlockSpec((halo, d), lambda i: (jnp.maximum(i * r - 1, 0), 0)),
                  pl.BlockSpec((1, d), lambda i: (0, 0)),
                  pl.BlockSpec((ng, gc, gc), lambda i: (0, 0, 0)),
                  pl.BlockSpec((1, d), lambda i: (0, 0))],
        out_specs=pl.BlockSpec((tm, d), lambda i: (i, 0)),
        out_shape=jax.ShapeDtypeStruct((m, d), F32),
        scratch_shapes=[pltpu.VMEM((tm + halo, d), F32)],
        compiler_params=_params("parallel"),
        name="pool_mixer",
    )(x, x, g.reshape(1, d), w_groups, scale.reshape(1, d))


def kernel(x, mix_norm, ffn_norm, final_norm, hyb_w_in, hyb_conv_a, hyb_conv_b,
           hyb_conv_b_bias, hyb_ln_g, hyb_ln_b, hyb_w_out, pool_w, pool_scale,
           ffn_w_up, ffn_conv, ffn_conv_bias, ffn_w_down):
    b, s, d = x.shape
    assert b == 1, "causal halos are carried across row tiles of a single sequence"
    depth = mix_norm.shape[0]
    d_a = hyb_conv_a.shape[2]
    d_b = hyb_conv_b.shape[2]
    t = _tiles(s)
    xs = x.reshape(s, d)

    for l in range(depth):
        i = l // 2
        if l % 2 == 0:
            h = _rmsnorm(xs, mix_norm[l], BF16, t["norm"])
            ya, u = _hyb_in(h, hyb_w_in[i].astype(BF16), hyb_conv_a[i], d_a, d_b, t["hyb_in"])
            yb = _convb(u, hyb_conv_b[i], hyb_conv_b_bias[i], hyb_ln_g[i], hyb_ln_b[i], t["convb"])
            xs = _mm2_res(ya, yb, hyb_w_out[i].astype(BF16), xs, t["down"], 2 * COL_TILE)
        else:
            xs = _pool_mixer(xs, mix_norm[l], pool_w[i].astype(BF16), pool_scale[i], t["pool"])
        h = _rmsnorm(xs, ffn_norm[l], BF16, t["norm"])
        act = _ffn_up(h, ffn_w_up[l].astype(BF16), ffn_conv[l], ffn_conv_bias[l], t["ffn_up"])
        xs = _mm_res(act, ffn_w_down[l].astype(BF16), xs, t["down"], COL_TILE)
    out = _rmsnorm(xs, final_norm, x.dtype, t["norm"])
    return out.reshape(b, s, d)
```

```python
import functools

import jax
import jax.numpy as jnp
from jax import lax
from jax.experimental import pallas as pl
from jax.experimental.pallas import tpu as pltpu

EPS = 1e-6
POOL_WINDOWS = (2, 4, 8, 16)
F32 = jnp.float32
BF16 = jnp.bfloat16

VMEM_LIMIT_BYTES = 56 * 1024 * 1024
SUBLANES = 8
LANES = 128
COL_TILE = 256
ROW_CHUNK = 256


def _tiles(m_rows):
    return dict(
        norm=min(256, m_rows),
        ffn_up=min(1024, m_rows),
        down=min(512, m_rows),
        out=min(512, m_rows),
        hyb_in=min(512, m_rows),
        convb=min(256, m_rows),
        pool=min(256, m_rows),
    )


def _params(*sem):
    return pltpu.CompilerParams(dimension_semantics=sem, vmem_limit_bytes=VMEM_LIMIT_BYTES)


def _sigmoid(v):
    return 1.0 / (1.0 + jnp.exp(-v))


def _rms(x, g):
    ms = jnp.mean(x * x, axis=-1, keepdims=True)
    return x * lax.rsqrt(ms + EPS) * g


def _rmsnorm_kernel(x_ref, g_ref, o_ref):
    o_ref[...] = _rms(x_ref[...], g_ref[...]).astype(o_ref.dtype)


def _rmsnorm(x, g, out_dtype, tm):
    m, d = x.shape
    return pl.pallas_call(
        _rmsnorm_kernel,
        grid=(m // tm,),
        in_specs=[pl.BlockSpec((tm, d), lambda i: (i, 0)),
                  pl.BlockSpec((1, d), lambda i: (0, 0))],
        out_specs=pl.BlockSpec((tm, d), lambda i: (i, 0)),
        out_shape=jax.ShapeDtypeStruct((m, d), out_dtype),
        compiler_params=_params("parallel"),
        name="rmsnorm",
    )(x, g.reshape(1, d))


def _conv3_chunk(u, taps, ext_ref, row0, rows):
    ext_ref[SUBLANES + row0:SUBLANES + row0 + rows, :] = u
    return (taps[2:3, :] * u
            + taps[1:2, :] * ext_ref[pl.ds(SUBLANES - 1 + row0, rows), :]
            + taps[0:1, :] * ext_ref[pl.ds(SUBLANES - 2 + row0, rows), :])


def _halo_load(halo_ref, ext_ref, m, n):
    @pl.when(m == 0)
    def _():
        halo_ref[n] = jnp.zeros(halo_ref.shape[1:], F32)

    ext_ref[0:SUBLANES, :] = halo_ref[n]


def _halo_save(halo_ref, ext_ref, n, tm):
    halo_ref[n] = ext_ref[tm:tm + SUBLANES, :]


def _ffn_up_kernel(h_ref, w_ref, cg_ref, cu_ref, bg_ref, bu_ref, o_ref,
                   halo_g, halo_u, ext_g, ext_u, *, tm, rc, tn):
    m = pl.program_id(0)
    n = pl.program_id(1)
    _halo_load(halo_g, ext_g, m, n)
    _halo_load(halo_u, ext_u, m, n)
    cg = cg_ref[...]
    cu = cu_ref[...]
    for c in range(tm // rc):
        r0 = c * rc
        u = jnp.dot(h_ref[r0:r0 + rc, :], w_ref[...], preferred_element_type=F32)
        gate = _conv3_chunk(u[:, :tn], cg, ext_g, r0, rc) + bg_ref[...]
        up = _conv3_chunk(u[:, tn:], cu, ext_u, r0, rc) + bu_ref[...]
        o_ref[r0:r0 + rc, :] = ((gate * _sigmoid(gate)) * up).astype(o_ref.dtype)
    _halo_save(halo_g, ext_g, n, tm)
    _halo_save(halo_u, ext_u, n, tm)


def _interleave_gate_up(w_up, tn):
    nl, d, f2 = w_up.shape
    nt = f2 // (2 * tn)
    return w_up.reshape(nl, d, 2, nt, tn).transpose(0, 1, 3, 2, 4).reshape(nl, d, f2)


def _ffn_up(h, w_gu, layer, conv, bias, tm):
    m, d = h.shape
    f = w_gu.shape[2] // 2
    tn = COL_TILE
    nt = f // tn
    bias2 = bias.reshape(1, 2 * f)
    kt = conv.shape[0]
    return pl.pallas_call(
        functools.partial(_ffn_up_kernel, tm=tm, rc=min(ROW_CHUNK, tm), tn=tn),
        grid=(m // tm, nt),
        in_specs=[
            pl.BlockSpec((tm, d), lambda i, j: (i, 0)),
            pl.BlockSpec((None, d, 2 * tn), lambda i, j: (layer, 0, j)),
            pl.BlockSpec((kt, tn), lambda i, j: (0, j)),
            pl.BlockSpec((kt, tn), lambda i, j: (0, nt + j)),
            pl.BlockSpec((1, tn), lambda i, j: (0, j)),
            pl.BlockSpec((1, tn), lambda i, j: (0, nt + j)),
        ],
        out_specs=pl.BlockSpec((tm, tn), lambda i, j: (i, j)),
        out_shape=jax.ShapeDtypeStruct((m, f), BF16),
        scratch_shapes=[
            pltpu.VMEM((nt, SUBLANES, tn), F32),
            pltpu.VMEM((nt, SUBLANES, tn), F32),
            pltpu.VMEM((tm + SUBLANES, tn), F32),
            pltpu.VMEM((tm + SUBLANES, tn), F32),
        ],
        compiler_params=_params("arbitrary", "arbitrary"),
        name="ffn_up",
    )(h, w_gu, conv, conv, bias2, bias2)


def _mm_res_kernel(a_ref, w_ref, x_ref, o_ref):
    o_ref[...] = x_ref[...] + jnp.dot(a_ref[...], w_ref[...], preferred_element_type=F32)


def _mm_res(a, w, layer, x, tm, tn):
    m, k = a.shape
    n = w.shape[2]
    return pl.pallas_call(
        _mm_res_kernel,
        grid=(m // tm, n // tn),
        in_specs=[pl.BlockSpec((tm, k), lambda i, j: (i, 0)),
                  pl.BlockSpec((None, k, tn), lambda i, j: (layer, 0, j)),
                  pl.BlockSpec((tm, tn), lambda i, j: (i, j))],
        out_specs=pl.BlockSpec((tm, tn), lambda i, j: (i, j)),
        out_shape=jax.ShapeDtypeStruct((m, n), F32),
        compiler_params=_params("parallel", "parallel"),
        name="mm_res",
    )(a, w, x)


def _mm2_res_kernel(a1_ref, a2_ref, w1_ref, w2_ref, x_ref, o_ref):
    acc = jnp.dot(a1_ref[...], w1_ref[...], preferred_element_type=F32)
    acc = acc + jnp.dot(a2_ref[...], w2_ref[...], preferred_element_type=F32)
    o_ref[...] = x_ref[...] + acc


def _mm2_res(a1, a2, w, layer, x, tm, tn):
    m, k1 = a1.shape
    k2 = a2.shape[1]
    assert k1 == k2 and w.shape[1] == k1 + k2
    n = w.shape[2]
    return pl.pallas_call(
        _mm2_res_kernel,
        grid=(m // tm, n // tn),
        in_specs=[pl.BlockSpec((tm, k1), lambda i, j: (i, 0)),
                  pl.BlockSpec((tm, k2), lambda i, j: (i, 0)),
                  pl.BlockSpec((None, k1, tn), lambda i, j: (layer, 0, j)),
                  pl.BlockSpec((None, k2, tn), lambda i, j: (layer, 1, j)),
                  pl.BlockSpec((tm, tn), lambda i, j: (i, j))],
        out_specs=pl.BlockSpec((tm, tn), lambda i, j: (i, j)),
        out_shape=jax.ShapeDtypeStruct((m, n), F32),
        compiler_params=_params("parallel", "parallel"),
        name="mm2_res",
    )(a1, a2, w, w, x)


def _hyb_in_kernel(h_ref, wx_ref, wc_ref, wb_ref, wv_ref, wg_ref, ca_ref,
                   ya_ref, u_ref, halo, ext, *, tm, rc):
    m = pl.program_id(0)
    n = pl.program_id(1)
    _halo_load(halo, ext, m, n)
    ca = ca_ref[...]
    for c in range(tm // rc):
        r0 = c * rc
        h = h_ref[r0:r0 + rc, :]
        dot = lambda w_ref: jnp.dot(h, w_ref[...], preferred_element_type=F32)
        z = dot(wc_ref) * dot(wx_ref)
        conv = _conv3_chunk(z, ca, ext, r0, rc)
        ya_ref[r0:r0 + rc, :] = (dot(wb_ref) * conv).astype(ya_ref.dtype)
        u_ref[r0:r0 + rc, :] = dot(wv_ref) * _sigmoid(dot(wg_ref))
    _halo_save(halo, ext, n, tm)


def _hyb_in(h, w_in, layer, conv_a, d_a, d_b, tm):
    m, d = h.shape
    assert d_a == d_b and w_in.shape[2] == 3 * d_a + 2 * d_b
    tn = COL_TILE
    nt = d_a // tn
    kt = conv_a.shape[0]
    wspec = lambda off: pl.BlockSpec((None, d, tn), lambda i, j: (layer, 0, off * nt + j))
    return pl.pallas_call(
        functools.partial(_hyb_in_kernel, tm=tm, rc=min(ROW_CHUNK, tm)),
        grid=(m // tm, nt),
        in_specs=[pl.BlockSpec((tm, d), lambda i, j: (i, 0)),
                  wspec(0), wspec(1), wspec(2), wspec(3), wspec(4),
                  pl.BlockSpec((kt, tn), lambda i, j: (0, j))],
        out_specs=[pl.BlockSpec((tm, tn), lambda i, j: (i, j)),
                   pl.BlockSpec((tm, tn), lambda i, j: (i, j))],
        out_shape=[jax.ShapeDtypeStruct((m, d_a), BF16),
                   jax.ShapeDtypeStruct((m, d_b), F32)],
        scratch_shapes=[pltpu.VMEM((nt, SUBLANES, tn), F32),
                        pltpu.VMEM((tm + SUBLANES, tn), F32)],
        compiler_params=_params("arbitrary", "arbitrary"),
        name="hyb_in",
    )(h, w_in, w_in, w_in, w_in, w_in, conv_a)


def _convb_kernel(u_ref, uh_ref, cw_ref, cb_ref, g_ref, b_ref, o_ref, ext, cv,
                  *, tm, halo, kw):
    m = pl.program_id(0)
    c = u_ref.shape[1]

    @pl.when(m == 0)
    def _():
        ext[0:halo, :] = jnp.zeros((halo, c), F32)

    @pl.when(m > 0)
    def _():
        ext[0:halo, :] = uh_ref[...]

    ext[halo:, :] = u_ref[...]
    base = halo - (kw - 1)

    def col_block(cb, carry):
        cols = pl.ds(pl.multiple_of(cb * LANES, LANES), LANES)
        acc = None
        for r in range(SUBLANES):
            rows = tm if r == 0 else tm + SUBLANES
            q = None
            for o in range(r, halo + 1, SUBLANES):
                j = o - base
                if 0 <= j < kw:
                    term = cw_ref[j:j + 1, cols] * ext[pl.ds(o - r, rows), cols]
                    q = term if q is None else q + term
            acc = q if r == 0 else acc + q[r:r + tm, :]
        cv[:, cols] = acc
        return carry

    lax.fori_loop(0, c // LANES, col_block, 0)
    v = cv[...] + cb_ref[...]
    mu = jnp.mean(v, axis=-1, keepdims=True)
    vc = v - mu
    y = vc * lax.rsqrt(jnp.mean(vc * vc, axis=-1, keepdims=True) + EPS)
    y = y * g_ref[...] + b_ref[...]
    o_ref[...] = (y * _sigmoid(y)).astype(o_ref.dtype)


def _convb(u, conv_w, conv_b, ln_g, ln_b, tm):
    m, c = u.shape
    kw = conv_w.shape[0]
    halo = -(-(kw - 1) // SUBLANES) * SUBLANES
    assert tm % halo == 0
    r = tm // halo
    row = lambda v: v.reshape(1, c)
    return pl.pallas_call(
        functools.partial(_convb_kernel, tm=tm, halo=halo, kw=kw),
        grid=(m // tm,),
        in_specs=[pl.BlockSpec((tm, c), lambda i: (i, 0)),
                  pl.BlockSpec((halo, c), lambda i: (jnp.maximum(i * r - 1, 0), 0)),
                  pl.BlockSpec((kw, c), lambda i: (0, 0)),
                  pl.BlockSpec((1, c), lambda i: (0, 0)),
                  pl.BlockSpec((1, c), lambda i: (0, 0)),
                  pl.BlockSpec((1, c), lambda i: (0, 0))],
        out_specs=pl.BlockSpec((tm, c), lambda i: (i, 0)),
        out_shape=jax.ShapeDtypeStruct((m, c), BF16),
        scratch_shapes=[pltpu.VMEM((tm + halo, c), F32),
                        pltpu.VMEM((tm, c), F32)],
        compiler_params=_params("parallel"),
        name="convb_ln",
    )(u, u, conv_w, row(conv_b), row(ln_g), row(ln_b))


def _pool_kernel(x_ref, xh_ref, g_ref, w_ref, s_ref, o_ref, he, *, tm, halo, windows):
    m = pl.program_id(0)
    g = g_ref[...]
    d = x_ref.shape[1]
    gc = d // len(windows)

    @pl.when(m == 0)
    def _():
        he[0:halo, :] = jnp.zeros((halo, d), F32)

    @pl.when(m > 0)
    def _():
        he[0:halo, :] = _rms(xh_ref[...], g)

    he[halo:, :] = _rms(x_ref[...], g)
    pos = m * tm + lax.broadcasted_iota(jnp.int32, (tm, 1), 0)
    for gi, win in enumerate(windows):
        cols = slice(gi * gc, (gi + 1) * gc)
        tok = he[pl.ds(halo, tm), cols]
        s = tok
        for k in range(1, win):
            s = s + he[pl.ds(halo - k, tm), cols]
        cnt = jnp.minimum(pos + 1, win).astype(F32)
        dp = s / cnt - tok
        y = jnp.dot(dp.astype(BF16), w_ref[gi], preferred_element_type=F32)
        o_ref[:, cols] = x_ref[:, cols] + y * s_ref[:, cols]


def _pool_mixer(x, g, w_groups, layer, scale, tm):
    m, d = x.shape
    _, ng, gc, _ = w_groups.shape
    windows = POOL_WINDOWS
    assert ng == len(windows) and ng * gc == d
    halo = -(-(max(windows) - 1) // SUBLANES) * SUBLANES
    assert tm % halo == 0
    r = tm // halo
    return pl.pallas_call(
        functools.partial(_pool_kernel, tm=tm, halo=halo, windows=windows),
        grid=(m // tm,),
        in_specs=[pl.BlockSpec((tm, d), lambda i: (i, 0)),
                  pl.BlockSpec((halo, d), lambda i: (jnp.maximum(i * r - 1, 0), 0)),
                  pl.BlockSpec((1, d), lambda i: (0, 0)),
                  pl.BlockSpec((None, ng, gc, gc), lambda i: (layer, 0, 0, 0)),
                  pl.BlockSpec((1, d), lambda i: (0, 0))],
        out_specs=pl.BlockSpec((tm, d), lambda i: (i, 0)),
        out_shape=jax.ShapeDtypeStruct((m, d), F32),
        scratch_shapes=[pltpu.VMEM((tm + halo, d), F32)],
        compiler_params=_params("parallel"),
        name="pool_mixer",
    )(x, x, g.reshape(1, d), w_groups, scale.reshape(1, d))


def kernel(x, mix_norm, ffn_norm, final_norm, hyb_w_in, hyb_conv_a, hyb_conv_b,
           hyb_conv_b_bias, hyb_ln_g, hyb_ln_b, hyb_w_out, pool_w, pool_scale,
           ffn_w_up, ffn_conv, ffn_conv_bias, ffn_w_down):
    b, s, d = x.shape
    assert b == 1, "causal halos are carried across row tiles of a single sequence"
    depth = mix_norm.shape[0]
    d_a = hyb_conv_a.shape[2]
    d_b = hyb_conv_b.shape[2]
    t = _tiles(s)
    xs = x.reshape(s, d)
    w_in = hyb_w_in.astype(BF16)
    w_out = hyb_w_out.astype(BF16)
    w_pool = pool_w.astype(BF16)
    w_gu = _interleave_gate_up(ffn_w_up.astype(BF16), COL_TILE)
    w_down = ffn_w_down.astype(BF16)

    for l in range(depth):
        i = l // 2
        if l % 2 == 0:
            h = _rmsnorm(xs, mix_norm[l], BF16, t["norm"])
            ya, u = _hyb_in(h, w_in, i, hyb_conv_a[i], d_a, d_b, t["hyb_in"])
            yb = _convb(u, hyb_conv_b[i], hyb_conv_b_bias[i], hyb_ln_g[i], hyb_ln_b[i], t["convb"])
            xs = _mm2_res(ya, yb, w_out, i, xs, t["out"], 2 * COL_TILE)
        else:
            xs = _pool_mixer(xs, mix_norm[l], w_pool, i, pool_scale[i], t["pool"])
        h = _rmsnorm(xs, ffn_norm[l], BF16, t["norm"])
        act = _ffn_up(h, w_gu, l, ffn_conv[l], ffn_conv_bias[l], t["ffn_up"])
        xs = _mm_res(act, w_down, l, xs, t["down"], 2 * COL_TILE)
    out = _rmsnorm(xs, final_norm, x.dtype, t["norm"])
    return out.reshape(b, s, d)
```

```python
import functools

import jax
import jax.numpy as jnp
from jax import lax
from jax.experimental import pallas as pl
from jax.experimental.pallas import tpu as pltpu

EPS = 1e-6
POOL_WINDOWS = (2, 4, 8, 16)
F32 = jnp.float32
BF16 = jnp.bfloat16

VMEM_LIMIT_BYTES = 56 * 1024 * 1024
SUBLANES = 8
LANES = 128
COL_TILE = 256
ROW_CHUNK = 256


def _tiles(m_rows):
    return dict(
        norm=min(256, m_rows),
        ffn_up=min(1024, m_rows),
        down=min(512, m_rows),
        out=min(512, m_rows),
        hyb_in=min(512, m_rows),
        convb=min(256, m_rows),
        pool=min(256, m_rows),
    )


def _params(*sem):
    return pltpu.CompilerParams(dimension_semantics=sem, vmem_limit_bytes=VMEM_LIMIT_BYTES)


def _sigmoid(v):
    return 1.0 / (1.0 + jnp.exp(-v))


def _rms(x, g):
    ms = jnp.mean(x * x, axis=-1, keepdims=True)
    return x * lax.rsqrt(ms + EPS) * g


def _lane_ssq(x):
    x2 = x * x
    acc = x2[:, :LANES]
    for k in range(1, x.shape[1] // LANES):
        acc = acc + x2[:, k * LANES:(k + 1) * LANES]
    return acc


def _lane_total(p):
    return jnp.broadcast_to(jnp.sum(p, axis=-1, keepdims=True), p.shape)


def _inv_rms(ssq, d, width):
    inv = lax.rsqrt(ssq / d + EPS)
    return jnp.concatenate([inv] * (width // LANES), axis=-1)


def _rmsnorm_kernel(x_ref, g_ref, o_ref):
    o_ref[...] = _rms(x_ref[...], g_ref[...]).astype(o_ref.dtype)


def _rmsnorm(x, g, out_dtype, tm):
    m, d = x.shape
    return pl.pallas_call(
        _rmsnorm_kernel,
        grid=(m // tm,),
        in_specs=[pl.BlockSpec((tm, d), lambda i: (i, 0)),
                  pl.BlockSpec((1, d), lambda i: (0, 0))],
        out_specs=pl.BlockSpec((tm, d), lambda i: (i, 0)),
        out_shape=jax.ShapeDtypeStruct((m, d), out_dtype),
        compiler_params=_params("parallel"),
        name="rmsnorm",
    )(x, g.reshape(1, d))


def _norm_prep_kernel(x_ref, g_ref, xg_ref, ssq_ref):
    x = x_ref[...]
    xg_ref[...] = (x * g_ref[...]).astype(xg_ref.dtype)
    ssq_ref[...] = _lane_total(_lane_ssq(x))


def _norm_prep(x, g, tm):
    m, d = x.shape
    return pl.pallas_call(
        _norm_prep_kernel,
        grid=(m // tm,),
        in_specs=[pl.BlockSpec((tm, d), lambda i: (i, 0)),
                  pl.BlockSpec((1, d), lambda i: (0, 0))],
        out_specs=[pl.BlockSpec((tm, d), lambda i: (i, 0)),
                   pl.BlockSpec((tm, LANES), lambda i: (i, 0))],
        out_shape=[jax.ShapeDtypeStruct((m, d), BF16),
                   jax.ShapeDtypeStruct((m, LANES), F32)],
        compiler_params=_params("parallel"),
        name="norm_prep",
    )(x, g.reshape(1, d))


def _conv3_chunk(u, taps, ext_ref, row0, rows):
    ext_ref[SUBLANES + row0:SUBLANES + row0 + rows, :] = u
    return (taps[2:3, :] * u
            + taps[1:2, :] * ext_ref[pl.ds(SUBLANES - 1 + row0, rows), :]
            + taps[0:1, :] * ext_ref[pl.ds(SUBLANES - 2 + row0, rows), :])


def _halo_load(halo_ref, ext_ref, m, n):
    @pl.when(m == 0)
    def _():
        halo_ref[n] = jnp.zeros(halo_ref.shape[1:], F32)

    ext_ref[0:SUBLANES, :] = halo_ref[n]


def _halo_save(halo_ref, ext_ref, n, tm):
    halo_ref[n] = ext_ref[tm:tm + SUBLANES, :]


def _ffn_up_kernel(xg_ref, ssq_ref, wg_ref, wu_ref, cg_ref, cu_ref, bg_ref, bu_ref, o_ref,
                   halo_g, halo_u, ext_g, ext_u, *, tm, rc):
    m = pl.program_id(0)
    n = pl.program_id(1)
    d = xg_ref.shape[1]
    _halo_load(halo_g, ext_g, m, n)
    _halo_load(halo_u, ext_u, m, n)
    cg = cg_ref[...]
    cu = cu_ref[...]
    for c in range(tm // rc):
        r0 = c * rc
        xg = xg_ref[r0:r0 + rc, :]
        inv = _inv_rms(ssq_ref[r0:r0 + rc, :], d, o_ref.shape[1])
        ug = jnp.dot(xg, wg_ref[...], preferred_element_type=F32) * inv
        uu = jnp.dot(xg, wu_ref[...], preferred_element_type=F32) * inv
        gate = _conv3_chunk(ug, cg, ext_g, r0, rc) + bg_ref[...]
        up = _conv3_chunk(uu, cu, ext_u, r0, rc) + bu_ref[...]
        o_ref[r0:r0 + rc, :] = ((gate * _sigmoid(gate)) * up).astype(o_ref.dtype)
    _halo_save(halo_g, ext_g, n, tm)
    _halo_save(halo_u, ext_u, n, tm)


def _ffn_up(xg, ssq, w_up, layer, conv, bias, tm):
    m, d = xg.shape
    f = w_up.shape[2] // 2
    tn = COL_TILE
    nt = f // tn
    bias2 = bias.reshape(1, 2 * f)
    kt = conv.shape[0]
    return pl.pallas_call(
        functools.partial(_ffn_up_kernel, tm=tm, rc=min(ROW_CHUNK, tm)),
        grid=(m // tm, nt),
        in_specs=[
            pl.BlockSpec((tm, d), lambda i, j: (i, 0)),
            pl.BlockSpec((tm, LANES), lambda i, j: (i, 0)),
            pl.BlockSpec((None, d, tn), lambda i, j: (layer, 0, j)),
            pl.BlockSpec((None, d, tn), lambda i, j: (layer, 0, nt + j)),
            pl.BlockSpec((kt, tn), lambda i, j: (0, j)),
            pl.BlockSpec((kt, tn), lambda i, j: (0, nt + j)),
            pl.BlockSpec((1, tn), lambda i, j: (0, j)),
            pl.BlockSpec((1, tn), lambda i, j: (0, nt + j)),
        ],
        out_specs=pl.BlockSpec((tm, tn), lambda i, j: (i, j)),
        out_shape=jax.ShapeDtypeStruct((m, f), BF16),
        scratch_shapes=[
            pltpu.VMEM((nt, SUBLANES, tn), F32),
            pltpu.VMEM((nt, SUBLANES, tn), F32),
            pltpu.VMEM((tm + SUBLANES, tn), F32),
            pltpu.VMEM((tm + SUBLANES, tn), F32),
        ],
        compiler_params=_params("arbitrary", "arbitrary"),
        name="ffn_up",
    )(xg, ssq, w_up, w_up, conv, conv, bias2, bias2)


def _mm_res_kernel(*refs, n_a, emit):
    a_refs, w_refs, x_ref = refs[:n_a], refs[n_a:2 * n_a], refs[2 * n_a]
    acc = None
    for a_ref, w_ref in zip(a_refs, w_refs):
        part = jnp.dot(a_ref[...], w_ref[...], preferred_element_type=F32)
        acc = part if acc is None else acc + part
    xn = x_ref[...] + acc
    if not emit:
        refs[-1][...] = xn
        return
    g_ref, o_ref, xg_ref, ssq_ref = refs[2 * n_a + 1:]
    o_ref[...] = xn
    xg_ref[...] = (xn * g_ref[...]).astype(xg_ref.dtype)
    j = pl.program_id(1)

    part = _lane_ssq(xn)

    @pl.when(j == 0)
    def _():
        ssq_ref[...] = part

    @pl.when(j > 0)
    def _():
        ssq_ref[...] += part

    @pl.when(j == pl.num_programs(1) - 1)
    def _():
        ssq_ref[...] = _lane_total(ssq_ref[...])


def _mm_res(a_list, w, layer, x, tm, tn, g_next=None):
    m = x.shape[0]
    n = w.shape[2]
    ks = [a.shape[1] for a in a_list]
    assert len(set(ks)) == 1 and w.shape[1] == sum(ks)
    k = ks[0]
    emit = g_next is not None
    in_specs = [pl.BlockSpec((tm, k), lambda i, j: (i, 0)) for _ in a_list]
    in_specs += [pl.BlockSpec((None, k, tn), lambda i, j, p=p: (layer, p, j)) for p in range(len(a_list))]
    in_specs += [pl.BlockSpec((tm, tn), lambda i, j: (i, j))]
    out_specs = [pl.BlockSpec((tm, tn), lambda i, j: (i, j))]
    out_shape = [jax.ShapeDtypeStruct((m, n), F32)]
    args = list(a_list) + [w] * len(a_list) + [x]
    if emit:
        in_specs += [pl.BlockSpec((1, tn), lambda i, j: (0, j))]
        args += [g_next.reshape(1, n)]
        out_specs += [pl.BlockSpec((tm, tn), lambda i, j: (i, j)),
                      pl.BlockSpec((tm, LANES), lambda i, j: (i, 0))]
        out_shape += [jax.ShapeDtypeStruct((m, n), BF16),
                      jax.ShapeDtypeStruct((m, LANES), F32)]
    outs = pl.pallas_call(
        functools.partial(_mm_res_kernel, n_a=len(a_list), emit=emit),
        grid=(m // tm, n // tn),
        in_specs=in_specs,
        out_specs=out_specs,
        out_shape=out_shape,
        compiler_params=_params("parallel", "arbitrary"),
        name="mm_res",
    )(*args)
    return outs if emit else outs[0]


def _hyb_in_kernel(xg_ref, ssq_ref, wx_ref, wc_ref, wb_ref, wv_ref, wg_ref, ca_ref,
                   ya_ref, u_ref, halo, ext, *, tm, rc):
    m = pl.program_id(0)
    n = pl.program_id(1)
    d = xg_ref.shape[1]
    _halo_load(halo, ext, m, n)
    ca = ca_ref[...]
    for c in range(tm // rc):
        r0 = c * rc
        xg = xg_ref[r0:r0 + rc, :]
        inv = _inv_rms(ssq_ref[r0:r0 + rc, :], d, ya_ref.shape[1])
        dot = lambda w_ref: jnp.dot(xg, w_ref[...], preferred_element_type=F32) * inv
        z = dot(wc_ref) * dot(wx_ref)
        conv = _conv3_chunk(z, ca, ext, r0, rc)
        ya_ref[r0:r0 + rc, :] = (dot(wb_ref) * conv).astype(ya_ref.dtype)
        u_ref[r0:r0 + rc, :] = dot(wv_ref) * _sigmoid(dot(wg_ref))
    _halo_save(halo, ext, n, tm)


def _hyb_in(xg, ssq, w_in, layer, conv_a, d_a, d_b, tm):
    m, d = xg.shape
    assert d_a == d_b and w_in.shape[2] == 3 * d_a + 2 * d_b
    tn = COL_TILE
    nt = d_a // tn
    kt = conv_a.shape[0]
    wspec = lambda off: pl.BlockSpec((None, d, tn), lambda i, j: (layer, 0, off * nt + j))
    return pl.pallas_call(
        functools.partial(_hyb_in_kernel, tm=tm, rc=min(ROW_CHUNK, tm)),
        grid=(m // tm, nt),
        in_specs=[pl.BlockSpec((tm, d), lambda i, j: (i, 0)),
                  pl.BlockSpec((tm, LANES), lambda i, j: (i, 0)),
                  wspec(0), wspec(1), wspec(2), wspec(3), wspec(4),
                  pl.BlockSpec((kt, tn), lambda i, j: (0, j))],
        out_specs=[pl.BlockSpec((tm, tn), lambda i, j: (i, j)),
                   pl.BlockSpec((tm, tn), lambda i, j: (i, j))],
        out_shape=[jax.ShapeDtypeStruct((m, d_a), BF16),
                   jax.ShapeDtypeStruct((m, d_b), F32)],
        scratch_shapes=[pltpu.VMEM((nt, SUBLANES, tn), F32),
                        pltpu.VMEM((tm + SUBLANES, tn), F32)],
        compiler_params=_params("arbitrary", "arbitrary"),
        name="hyb_in",
    )(xg, ssq, w_in, w_in, w_in, w_in, w_in, conv_a)


def _convb_kernel(u_ref, uh_ref, cw_ref, cb_ref, g_ref, b_ref, o_ref, ext, cv,
                  *, tm, halo, kw):
    m = pl.program_id(0)
    c = u_ref.shape[1]

    @pl.when(m == 0)
    def _():
        ext[0:halo, :] = jnp.zeros((halo, c), F32)

    @pl.when(m > 0)
    def _():
        ext[0:halo, :] = uh_ref[...]

    ext[halo:, :] = u_ref[...]
    base = halo - (kw - 1)

    def col_block(cb, carry):
        cols = pl.ds(pl.multiple_of(cb * LANES, LANES), LANES)
        acc = None
        for r in range(SUBLANES):
            rows = tm if r == 0 else tm + SUBLANES
            q = None
            for o in range(r, halo + 1, SUBLANES):
                j = o - base
                if 0 <= j < kw:
                    term = cw_ref[j:j + 1, cols] * ext[pl.ds(o - r, rows), cols]
                    q = term if q is None else q + term
            acc = q if r == 0 else acc + q[r:r + tm, :]
        cv[:, cols] = acc
        return carry

    lax.fori_loop(0, c // LANES, col_block, 0)
    v = cv[...] + cb_ref[...]
    mu = jnp.mean(v, axis=-1, keepdims=True)
    vc = v - mu
    y = vc * lax.rsqrt(jnp.mean(vc * vc, axis=-1, keepdims=True) + EPS)
    y = y * g_ref[...] + b_ref[...]
    o_ref[...] = (y * _sigmoid(y)).astype(o_ref.dtype)


def _convb(u, conv_w, conv_b, ln_g, ln_b, tm):
    m, c = u.shape
    kw = conv_w.shape[0]
    halo = -(-(kw - 1) // SUBLANES) * SUBLANES
    assert tm % halo == 0
    r = tm // halo
    row = lambda v: v.reshape(1, c)
    return pl.pallas_call(
        functools.partial(_convb_kernel, tm=tm, halo=halo, kw=kw),
        grid=(m // tm,),
        in_specs=[pl.BlockSpec((tm, c), lambda i: (i, 0)),
                  pl.BlockSpec((halo, c), lambda i: (jnp.maximum(i * r - 1, 0), 0)),
                  pl.BlockSpec((kw, c), lambda i: (0, 0)),
                  pl.BlockSpec((1, c), lambda i: (0, 0)),
                  pl.BlockSpec((1, c), lambda i: (0, 0)),
                  pl.BlockSpec((1, c), lambda i: (0, 0))],
        out_specs=pl.BlockSpec((tm, c), lambda i: (i, 0)),
        out_shape=jax.ShapeDtypeStruct((m, c), BF16),
        scratch_shapes=[pltpu.VMEM((tm + halo, c), F32),
                        pltpu.VMEM((tm, c), F32)],
        compiler_params=_params("parallel"),
        name="convb_ln",
    )(u, u, conv_w, row(conv_b), row(ln_g), row(ln_b))


def _pool_kernel(x_ref, xh_ref, g_ref, w_ref, s_ref, gn_ref, o_ref, xg_ref, ssq_ref, he,
                 *, tm, halo, windows):
    m = pl.program_id(0)
    g = g_ref[...]
    d = x_ref.shape[1]
    gc = d // len(windows)

    @pl.when(m == 0)
    def _():
        he[0:halo, :] = jnp.zeros((halo, d), F32)

    @pl.when(m > 0)
    def _():
        he[0:halo, :] = _rms(xh_ref[...], g)

    he[halo:, :] = _rms(x_ref[...], g)
    pos = m * tm + lax.broadcasted_iota(jnp.int32, (tm, 1), 0)
    ssq = jnp.zeros(ssq_ref.shape, F32)
    for gi, win in enumerate(windows):
        cols = slice(gi * gc, (gi + 1) * gc)
        tok = he[pl.ds(halo, tm), cols]
        s = tok
        for k in range(1, win):
            s = s + he[pl.ds(halo - k, tm), cols]
        cnt = jnp.minimum(pos + 1, win).astype(F32)
        dp = s / cnt - tok
        y = jnp.dot(dp.astype(BF16), w_ref[gi], preferred_element_type=F32)
        xn = x_ref[:, cols] + y * s_ref[:, cols]
        o_ref[:, cols] = xn
        xg_ref[:, cols] = (xn * gn_ref[:, cols]).astype(xg_ref.dtype)
        ssq = ssq + _lane_ssq(xn)
    ssq_ref[...] = _lane_total(ssq)


def _pool_mixer(x, g, w_groups, layer, scale, g_next, tm):
    m, d = x.shape
    _, ng, gc, _ = w_groups.shape
    windows = POOL_WINDOWS
    assert ng == len(windows) and ng * gc == d
    halo = -(-(max(windows) - 1) // SUBLANES) * SUBLANES
    assert tm % halo == 0
    r = tm // halo
    return pl.pallas_call(
        functools.partial(_pool_kernel, tm=tm, halo=halo, windows=windows),
        grid=(m // tm,),
        in_specs=[pl.BlockSpec((tm, d), lambda i: (i, 0)),
                  pl.BlockSpec((halo, d), lambda i: (jnp.maximum(i * r - 1, 0), 0)),
                  pl.BlockSpec((1, d), lambda i: (0, 0)),
                  pl.BlockSpec((None, ng, gc, gc), lambda i: (layer, 0, 0, 0)),
                  pl.BlockSpec((1, d), lambda i: (0, 0)),
                  pl.BlockSpec((1, d), lambda i: (0, 0))],
        out_specs=[pl.BlockSpec((tm, d), lambda i: (i, 0)),
                   pl.BlockSpec((tm, d), lambda i: (i, 0)),
                   pl.BlockSpec((tm, LANES), lambda i: (i, 0))],
        out_shape=[jax.ShapeDtypeStruct((m, d), F32),
                   jax.ShapeDtypeStruct((m, d), BF16),
                   jax.ShapeDtypeStruct((m, LANES), F32)],
        scratch_shapes=[pltpu.VMEM((tm + halo, d), F32)],
        compiler_params=_params("parallel"),
        name="pool_mixer",
    )(x, x, g.reshape(1, d), w_groups, scale.reshape(1, d), g_next.reshape(1, d))


def kernel(x, mix_norm, ffn_norm, final_norm, hyb_w_in, hyb_conv_a, hyb_conv_b,
           hyb_conv_b_bias, hyb_ln_g, hyb_ln_b, hyb_w_out, pool_w, pool_scale,
           ffn_w_up, ffn_conv, ffn_conv_bias, ffn_w_down):
    b, s, d = x.shape
    assert b == 1, "causal halos are carried across row tiles of a single sequence"
    depth = mix_norm.shape[0]
    d_a = hyb_conv_a.shape[2]
    d_b = hyb_conv_b.shape[2]
    t = _tiles(s)
    xs = x.reshape(s, d)
    w_in = hyb_w_in.astype(BF16)
    w_out = hyb_w_out.astype(BF16)
    w_pool = pool_w.astype(BF16)
    w_up = ffn_w_up.astype(BF16)
    w_down = ffn_w_down.astype(BF16)

    xg, ssq = _norm_prep(xs, mix_norm[0], t["norm"])
    for l in range(depth):
        i = l // 2
        if l % 2 == 0:
            ya, u = _hyb_in(xg, ssq, w_in, i, hyb_conv_a[i], d_a, d_b, t["hyb_in"])
            yb = _convb(u, hyb_conv_b[i], hyb_conv_b_bias[i], hyb_ln_g[i], hyb_ln_b[i], t["convb"])
            xs, xg, ssq = _mm_res([ya, yb], w_out, i, xs, t["out"], 2 * COL_TILE, g_next=ffn_norm[l])
        else:
            xs, xg, ssq = _pool_mixer(xs, mix_norm[l], w_pool, i, pool_scale[i], ffn_norm[l], t["pool"])
        act = _ffn_up(xg, ssq, w_up, l, ffn_conv[l], ffn_conv_bias[l], t["ffn_up"])
        if l + 1 < depth and (l + 1) % 2 == 0:
            xs, xg, ssq = _mm_res([act], w_down, l, xs, t["down"], 2 * COL_TILE, g_next=mix_norm[l + 1])
        else:
            xs = _mm_res([act], w_down, l, xs, t["down"], 2 * COL_TILE)
    out = _rmsnorm(xs, final_norm, x.dtype, t["norm"])
    return out.reshape(b, s, d)
```

```python
import functools

import jax
import jax.numpy as jnp
from jax import lax
from jax.experimental import pallas as pl
from jax.experimental.pallas import tpu as pltpu

EPS = 1e-6
POOL_WINDOWS = (2, 4, 8, 16)
F32 = jnp.float32
BF16 = jnp.bfloat16

VMEM_LIMIT_BYTES = 56 * 1024 * 1024
SUBLANES = 8
LANES = 128
COL_TILE = 256
ROW_CHUNK = 256


def _tiles(m_rows):
    return dict(
        norm=min(256, m_rows),
        ffn_up=min(2048, m_rows),
        down=min(512, m_rows),
        out=min(512, m_rows),
        hyb_in=min(1024, m_rows),
        convb=min(256, m_rows),
        pool=min(256, m_rows),
    )


def _params(*sem):
    return pltpu.CompilerParams(dimension_semantics=sem, vmem_limit_bytes=VMEM_LIMIT_BYTES)


def _sigmoid(v):
    return 1.0 / (1.0 + jnp.exp(-v))


def _rms(x, g):
    ms = jnp.mean(x * x, axis=-1, keepdims=True)
    return x * lax.rsqrt(ms + EPS) * g


def _lane_ssq(x):
    x2 = x * x
    acc = x2[:, :LANES]
    for k in range(1, x.shape[1] // LANES):
        acc = acc + x2[:, k * LANES:(k + 1) * LANES]
    return acc


def _lane_total(p):
    return jnp.broadcast_to(jnp.sum(p, axis=-1, keepdims=True), p.shape)


def _inv_rms(ssq, d, width):
    inv = lax.rsqrt(ssq / d + EPS)
    return jnp.concatenate([inv] * (width // LANES), axis=-1)


def _side_cast_specs(src, layer, n_steps, step_of):
    _, rows, cols = src.shape
    slab = rows // n_steps
    assert slab * n_steps == rows and slab % (2 * SUBLANES) == 0
    in_spec = pl.BlockSpec((None, slab, cols), lambda *g: (layer, step_of(*g), 0))
    out_spec = pl.BlockSpec((slab, cols), lambda *g: (step_of(*g), 0))
    return in_spec, out_spec, jax.ShapeDtypeStruct((rows, cols), BF16)


def _side_cast(src_ref, dst_ref):
    dst_ref[...] = src_ref[...].astype(dst_ref.dtype)


def _rmsnorm_kernel(x_ref, g_ref, o_ref):
    o_ref[...] = _rms(x_ref[...], g_ref[...]).astype(o_ref.dtype)


def _rmsnorm(x, g, out_dtype, tm):
    m, d = x.shape
    return pl.pallas_call(
        _rmsnorm_kernel,
        grid=(m // tm,),
        in_specs=[pl.BlockSpec((tm, d), lambda i: (i, 0)),
                  pl.BlockSpec((1, d), lambda i: (0, 0))],
        out_specs=pl.BlockSpec((tm, d), lambda i: (i, 0)),
        out_shape=jax.ShapeDtypeStruct((m, d), out_dtype),
        compiler_params=_params("parallel"),
        name="rmsnorm",
    )(x, g.reshape(1, d))


def _norm_prep_kernel(x_ref, g_ref, xg_ref, ssq_ref):
    x = x_ref[...]
    xg_ref[...] = (x * g_ref[...]).astype(xg_ref.dtype)
    ssq_ref[...] = _lane_total(_lane_ssq(x))


def _norm_prep(x, g, tm):
    m, d = x.shape
    return pl.pallas_call(
        _norm_prep_kernel,
        grid=(m // tm,),
        in_specs=[pl.BlockSpec((tm, d), lambda i: (i, 0)),
                  pl.BlockSpec((1, d), lambda i: (0, 0))],
        out_specs=[pl.BlockSpec((tm, d), lambda i: (i, 0)),
                   pl.BlockSpec((tm, LANES), lambda i: (i, 0))],
        out_shape=[jax.ShapeDtypeStruct((m, d), BF16),
                   jax.ShapeDtypeStruct((m, LANES), F32)],
        compiler_params=_params("parallel"),
        name="norm_prep",
    )(x, g.reshape(1, d))


def _conv3_chunk(u, taps, ext_ref, row0, rows):
    ext_ref[SUBLANES + row0:SUBLANES + row0 + rows, :] = u
    return (taps[2:3, :] * u
            + taps[1:2, :] * ext_ref[pl.ds(SUBLANES - 1 + row0, rows), :]
            + taps[0:1, :] * ext_ref[pl.ds(SUBLANES - 2 + row0, rows), :])


def _halo_load(halo_ref, ext_ref, m, n):
    @pl.when(m == 0)
    def _():
        halo_ref[n] = jnp.zeros(halo_ref.shape[1:], F32)

    ext_ref[0:SUBLANES, :] = halo_ref[n]


def _halo_save(halo_ref, ext_ref, n, tm):
    halo_ref[n] = ext_ref[tm:tm + SUBLANES, :]


def _ffn_up_kernel(xg_ref, ssq_ref, wg_ref, wu_ref, cg_ref, cu_ref, bg_ref, bu_ref, src_ref,
                   o_ref, dst_ref, halo_g, halo_u, ext_g, ext_u, *, tm, rc):
    m = pl.program_id(0)
    n = pl.program_id(1)
    d = xg_ref.shape[1]
    _side_cast(src_ref, dst_ref)
    _halo_load(halo_g, ext_g, m, n)
    _halo_load(halo_u, ext_u, m, n)
    cg = cg_ref[...]
    cu = cu_ref[...]
    for c in range(tm // rc):
        r0 = c * rc
        xg = xg_ref[r0:r0 + rc, :]
        inv = _inv_rms(ssq_ref[r0:r0 + rc, :], d, o_ref.shape[1])
        ug = jnp.dot(xg, wg_ref[...], preferred_element_type=F32) * inv
        uu = jnp.dot(xg, wu_ref[...], preferred_element_type=F32) * inv
        gate = _conv3_chunk(ug, cg, ext_g, r0, rc) + bg_ref[...]
        up = _conv3_chunk(uu, cu, ext_u, r0, rc) + bu_ref[...]
        o_ref[r0:r0 + rc, :] = ((gate * _sigmoid(gate)) * up).astype(o_ref.dtype)
    _halo_save(halo_g, ext_g, n, tm)
    _halo_save(halo_u, ext_u, n, tm)


def _ffn_up(xg, ssq, w_up, conv, bias, cast_src, layer, tm):
    m, d = xg.shape
    f = w_up.shape[1] // 2
    tn = COL_TILE
    nt = f // tn
    bias2 = bias.reshape(1, 2 * f)
    kt = conv.shape[0]
    c_in, c_out, c_shape = _side_cast_specs(cast_src, layer, (m // tm) * nt, lambda i, j: i * nt + j)
    return pl.pallas_call(
        functools.partial(_ffn_up_kernel, tm=tm, rc=min(ROW_CHUNK, tm)),
        grid=(m // tm, nt),
        in_specs=[
            pl.BlockSpec((tm, d), lambda i, j: (i, 0)),
            pl.BlockSpec((tm, LANES), lambda i, j: (i, 0)),
            pl.BlockSpec((d, tn), lambda i, j: (0, j)),
            pl.BlockSpec((d, tn), lambda i, j: (0, nt + j)),
            pl.BlockSpec((kt, tn), lambda i, j: (0, j)),
            pl.BlockSpec((kt, tn), lambda i, j: (0, nt + j)),
            pl.BlockSpec((1, tn), lambda i, j: (0, j)),
            pl.BlockSpec((1, tn), lambda i, j: (0, nt + j)),
            c_in,
        ],
        out_specs=[pl.BlockSpec((tm, tn), lambda i, j: (i, j)), c_out],
        out_shape=[jax.ShapeDtypeStruct((m, f), BF16), c_shape],
        scratch_shapes=[
            pltpu.VMEM((nt, SUBLANES, tn), F32),
            pltpu.VMEM((nt, SUBLANES, tn), F32),
            pltpu.VMEM((tm + SUBLANES, tn), F32),
            pltpu.VMEM((tm + SUBLANES, tn), F32),
        ],
        compiler_params=_params("arbitrary", "arbitrary"),
        name="ffn_up",
    )(xg, ssq, w_up, w_up, conv, conv, bias2, bias2, cast_src)


def _mm_res_kernel(*refs, n_a, emit):
    a_refs, w_refs, x_ref = refs[:n_a], refs[n_a:2 * n_a], refs[2 * n_a]
    acc = None
    for a_ref, w_ref in zip(a_refs, w_refs):
        part = jnp.dot(a_ref[...], w_ref[...], preferred_element_type=F32)
        acc = part if acc is None else acc + part
    xn = x_ref[...] + acc
    if not emit:
        refs[-1][...] = xn
        return
    g_ref, o_ref, xg_ref, ssq_ref = refs[2 * n_a + 1:]
    o_ref[...] = xn
    xg_ref[...] = (xn * g_ref[...]).astype(xg_ref.dtype)
    j = pl.program_id(1)

    part = _lane_ssq(xn)

    @pl.when(j == 0)
    def _():
        ssq_ref[...] = part

    @pl.when(j > 0)
    def _():
        ssq_ref[...] += part

    @pl.when(j == pl.num_programs(1) - 1)
    def _():
        ssq_ref[...] = _lane_total(ssq_ref[...])


def _mm_res(a_list, w, layer, x, tm, tn, g_next=None):
    m = x.shape[0]
    n = w.shape[-1]
    ks = [a.shape[1] for a in a_list]
    assert len(set(ks)) == 1 and w.shape[-2] == sum(ks) and (layer is None) == (w.ndim == 2)
    k = ks[0]
    emit = g_next is not None
    in_specs = [pl.BlockSpec((tm, k), lambda i, j: (i, 0)) for _ in a_list]
    if layer is None:
        in_specs += [pl.BlockSpec((k, tn), lambda i, j, p=p: (p, j)) for p in range(len(a_list))]
    else:
        in_specs += [pl.BlockSpec((None, k, tn), lambda i, j, p=p: (layer, p, j)) for p in range(len(a_list))]
    in_specs += [pl.BlockSpec((tm, tn), lambda i, j: (i, j))]
    out_specs = [pl.BlockSpec((tm, tn), lambda i, j: (i, j))]
    out_shape = [jax.ShapeDtypeStruct((m, n), F32)]
    args = list(a_list) + [w] * len(a_list) + [x]
    if emit:
        in_specs += [pl.BlockSpec((1, tn), lambda i, j: (0, j))]
        args += [g_next.reshape(1, n)]
        out_specs += [pl.BlockSpec((tm, tn), lambda i, j: (i, j)),
                      pl.BlockSpec((tm, LANES), lambda i, j: (i, 0))]
        out_shape += [jax.ShapeDtypeStruct((m, n), BF16),
                      jax.ShapeDtypeStruct((m, LANES), F32)]
    outs = pl.pallas_call(
        functools.partial(_mm_res_kernel, n_a=len(a_list), emit=emit),
        grid=(m // tm, n // tn),
        in_specs=in_specs,
        out_specs=out_specs,
        out_shape=out_shape,
        compiler_params=_params("parallel", "arbitrary"),
        name="mm_res",
    )(*args)
    return outs if emit else outs[0]


def _hyb_in_kernel(xg_ref, ssq_ref, wx_ref, wc_ref, wb_ref, wv_ref, wg_ref, ca_ref, src_ref,
                   ya_ref, u_ref, dst_ref, halo, ext, *, tm, rc):
    m = pl.program_id(0)
    n = pl.program_id(1)
    d = xg_ref.shape[1]
    _side_cast(src_ref, dst_ref)
    _halo_load(halo, ext, m, n)
    ca = ca_ref[...]
    for c in range(tm // rc):
        r0 = c * rc
        xg = xg_ref[r0:r0 + rc, :]
        inv = _inv_rms(ssq_ref[r0:r0 + rc, :], d, ya_ref.shape[1])
        dot = lambda w_ref: jnp.dot(xg, w_ref[...], preferred_element_type=F32) * inv
        z = dot(wc_ref) * dot(wx_ref)
        conv = _conv3_chunk(z, ca, ext, r0, rc)
        ya_ref[r0:r0 + rc, :] = (dot(wb_ref) * conv).astype(ya_ref.dtype)
        u_ref[r0:r0 + rc, :] = dot(wv_ref) * _sigmoid(dot(wg_ref))
    _halo_save(halo, ext, n, tm)


def _hyb_in(xg, ssq, w_in, layer, conv_a, d_a, d_b, cast_src, cast_layer, tm):
    m, d = xg.shape
    assert d_a == d_b and w_in.shape[2] == 3 * d_a + 2 * d_b
    tn = COL_TILE
    nt = d_a // tn
    kt = conv_a.shape[0]
    wspec = lambda off: pl.BlockSpec((None, d, tn), lambda i, j: (layer, 0, off * nt + j))
    c_in, c_out, c_shape = _side_cast_specs(cast_src, cast_layer, (m // tm) * nt, lambda i, j: i * nt + j)
    return pl.pallas_call(
        functools.partial(_hyb_in_kernel, tm=tm, rc=min(ROW_CHUNK, tm)),
        grid=(m // tm, nt),
        in_specs=[pl.BlockSpec((tm, d), lambda i, j: (i, 0)),
                  pl.BlockSpec((tm, LANES), lambda i, j: (i, 0)),
                  wspec(0), wspec(1), wspec(2), wspec(3), wspec(4),
                  pl.BlockSpec((kt, tn), lambda i, j: (0, j)),
                  c_in],
        out_specs=[pl.BlockSpec((tm, tn), lambda i, j: (i, j)),
                   pl.BlockSpec((tm, tn), lambda i, j: (i, j)),
                   c_out],
        out_shape=[jax.ShapeDtypeStruct((m, d_a), BF16),
                   jax.ShapeDtypeStruct((m, d_b), F32),
                   c_shape],
        scratch_shapes=[pltpu.VMEM((nt, SUBLANES, tn), F32),
                        pltpu.VMEM((tm + SUBLANES, tn), F32)],
        compiler_params=_params("arbitrary", "arbitrary"),
        name="hyb_in",
    )(xg, ssq, w_in, w_in, w_in, w_in, w_in, conv_a, cast_src)


def _convb_kernel(u_ref, uh_ref, cw_ref, cb_ref, g_ref, b_ref, src_ref, o_ref, dst_ref, ext, cv,
                  *, tm, halo, kw):
    m = pl.program_id(0)
    c = u_ref.shape[1]
    _side_cast(src_ref, dst_ref)

    @pl.when(m == 0)
    def _():
        ext[0:halo, :] = jnp.zeros((halo, c), F32)

    @pl.when(m > 0)
    def _():
        ext[0:halo, :] = uh_ref[...]

    ext[halo:, :] = u_ref[...]
    base = halo - (kw - 1)

    def col_block(cb, carry):
        cols = pl.ds(pl.multiple_of(cb * LANES, LANES), LANES)
        acc = None
        for r in range(SUBLANES):
            rows = tm if r == 0 else tm + SUBLANES
            q = None
            for o in range(r, halo + 1, SUBLANES):
                j = o - base
                if 0 <= j < kw:
                    term = cw_ref[j:j + 1, cols] * ext[pl.ds(o - r, rows), cols]
                    q = term if q is None else q + term
            acc = q if r == 0 else acc + q[r:r + tm, :]
        cv[:, cols] = acc
        return carry

    lax.fori_loop(0, c // LANES, col_block, 0)
    v = cv[...] + cb_ref[...]
    mu = jnp.mean(v, axis=-1, keepdims=True)
    vc = v - mu
    y = vc * lax.rsqrt(jnp.mean(vc * vc, axis=-1, keepdims=True) + EPS)
    y = y * g_ref[...] + b_ref[...]
    o_ref[...] = (y * _sigmoid(y)).astype(o_ref.dtype)


def _convb(u, conv_w, conv_b, ln_g, ln_b, cast_src, cast_layer, tm):
    m, c = u.shape
    kw = conv_w.shape[0]
    halo = -(-(kw - 1) // SUBLANES) * SUBLANES
    assert tm % halo == 0
    r = tm // halo
    row = lambda v: v.reshape(1, c)
    c_in, c_out, c_shape = _side_cast_specs(cast_src, cast_layer, m // tm, lambda i: i)
    return pl.pallas_call(
        functools.partial(_convb_kernel, tm=tm, halo=halo, kw=kw),
        grid=(m // tm,),
        in_specs=[pl.BlockSpec((tm, c), lambda i: (i, 0)),
                  pl.BlockSpec((halo, c), lambda i: (jnp.maximum(i * r - 1, 0), 0)),
                  pl.BlockSpec((kw, c), lambda i: (0, 0)),
                  pl.BlockSpec((1, c), lambda i: (0, 0)),
                  pl.BlockSpec((1, c), lambda i: (0, 0)),
                  pl.BlockSpec((1, c), lambda i: (0, 0)),
                  c_in],
        out_specs=[pl.BlockSpec((tm, c), lambda i: (i, 0)), c_out],
        out_shape=[jax.ShapeDtypeStruct((m, c), BF16), c_shape],
        scratch_shapes=[pltpu.VMEM((tm + halo, c), F32),
                        pltpu.VMEM((tm, c), F32)],
        compiler_params=_params("parallel"),
        name="convb_ln",
    )(u, u, conv_w, row(conv_b), row(ln_g), row(ln_b), cast_src)


def _pool_kernel(x_ref, xh_ref, g_ref, w_ref, s_ref, gn_ref, o_ref, xg_ref, ssq_ref, he,
                 *, tm, halo, windows):
    m = pl.program_id(0)
    g = g_ref[...]
    d = x_ref.shape[1]
    gc = d // len(windows)

    @pl.when(m == 0)
    def _():
        he[0:halo, :] = jnp.zeros((halo, d), F32)

    @pl.when(m > 0)
    def _():
        he[0:halo, :] = _rms(xh_ref[...], g)

    he[halo:, :] = _rms(x_ref[...], g)
    pos = m * tm + lax.broadcasted_iota(jnp.int32, (tm, 1), 0)
    ssq = jnp.zeros(ssq_ref.shape, F32)
    for gi, win in enumerate(windows):
        cols = slice(gi * gc, (gi + 1) * gc)
        tok = he[pl.ds(halo, tm), cols]
        s = tok
        for k in range(1, win):
            s = s + he[pl.ds(halo - k, tm), cols]
        cnt = jnp.minimum(pos + 1, win).astype(F32)
        dp = s / cnt - tok
        y = jnp.dot(dp.astype(BF16), w_ref[gi], preferred_element_type=F32)
        xn = x_ref[:, cols] + y * s_ref[:, cols]
        o_ref[:, cols] = xn
        xg_ref[:, cols] = (xn * gn_ref[:, cols]).astype(xg_ref.dtype)
        ssq = ssq + _lane_ssq(xn)
    ssq_ref[...] = _lane_total(ssq)


def _pool_mixer(x, g, w_groups, layer, scale, g_next, tm):
    m, d = x.shape
    _, ng, gc, _ = w_groups.shape
    windows = POOL_WINDOWS
    assert ng == len(windows) and ng * gc == d
    halo = -(-(max(windows) - 1) // SUBLANES) * SUBLANES
    assert tm % halo == 0
    r = tm // halo
    return pl.pallas_call(
        functools.partial(_pool_kernel, tm=tm, halo=halo, windows=windows),
        grid=(m // tm,),
        in_specs=[pl.BlockSpec((tm, d), lambda i: (i, 0)),
                  pl.BlockSpec((halo, d), lambda i: (jnp.maximum(i * r - 1, 0), 0)),
                  pl.BlockSpec((1, d), lambda i: (0, 0)),
                  pl.BlockSpec((None, ng, gc, gc), lambda i: (layer, 0, 0, 0)),
                  pl.BlockSpec((1, d), lambda i: (0, 0)),
                  pl.BlockSpec((1, d), lambda i: (0, 0))],
        out_specs=[pl.BlockSpec((tm, d), lambda i: (i, 0)),
                   pl.BlockSpec((tm, d), lambda i: (i, 0)),
                   pl.BlockSpec((tm, LANES), lambda i: (i, 0))],
        out_shape=[jax.ShapeDtypeStruct((m, d), F32),
                   jax.ShapeDtypeStruct((m, d), BF16),
                   jax.ShapeDtypeStruct((m, LANES), F32)],
        scratch_shapes=[pltpu.VMEM((tm + halo, d), F32)],
        compiler_params=_params("parallel"),
        name="pool_mixer",
    )(x, x, g.reshape(1, d), w_groups, scale.reshape(1, d), g_next.reshape(1, d))


def kernel(x, mix_norm, ffn_norm, final_norm, hyb_w_in, hyb_conv_a, hyb_conv_b,
           hyb_conv_b_bias, hyb_ln_g, hyb_ln_b, hyb_w_out, pool_w, pool_scale,
           ffn_w_up, ffn_conv, ffn_conv_bias, ffn_w_down):
    b, s, d = x.shape
    assert b == 1, "causal halos are carried across row tiles of a single sequence"
    depth = mix_norm.shape[0]
    d_a = hyb_conv_a.shape[2]
    d_b = hyb_conv_b.shape[2]
    t = _tiles(s)
    xs = x.reshape(s, d)
    w_in = hyb_w_in.astype(BF16)
    w_out = hyb_w_out.astype(BF16)
    w_pool = pool_w.astype(BF16)
    assert depth % 2 == 0, "odd layers' FFN weights are cast inside the preceding even layer"

    xg, ssq = _norm_prep(xs, mix_norm[0], t["norm"])
    w_up_next = None
    for l in range(depth):
        i = l // 2
        if l % 2 == 0:
            ya, u, w_up = _hyb_in(xg, ssq, w_in, i, hyb_conv_a[i], d_a, d_b, ffn_w_up, l, t["hyb_in"])
            yb, w_up_next = _convb(u, hyb_conv_b[i], hyb_conv_b_bias[i], hyb_ln_g[i], hyb_ln_b[i],
                                   ffn_w_up, l + 1, t["convb"])
            xs, xg, ssq = _mm_res([ya, yb], w_out, i, xs, t["out"], 2 * COL_TILE, g_next=ffn_norm[l])
        else:
            w_up = w_up_next
            xs, xg, ssq = _pool_mixer(xs, mix_norm[l], w_pool, i, pool_scale[i], ffn_norm[l], t["pool"])
        act, w_down = _ffn_up(xg, ssq, w_up, ffn_conv[l], ffn_conv_bias[l], ffn_w_down, l, t["ffn_up"])
        if l + 1 < depth and (l + 1) % 2 == 0:
            xs, xg, ssq = _mm_res([act], w_down, None, xs, t["down"], 2 * COL_TILE, g_next=mix_norm[l + 1])
        else:
            xs = _mm_res([act], w_down, None, xs, t["down"], 2 * COL_TILE)
    out = _rmsnorm(xs, final_norm, x.dtype, t["norm"])
    return out.reshape(b, s, d)
```

```python
import functools

import jax
import jax.numpy as jnp
from jax import lax
from jax.experimental import pallas as pl
from jax.experimental.pallas import tpu as pltpu

EPS = 1e-6
POOL_WINDOWS = (2, 4, 8, 16)
F32 = jnp.float32
BF16 = jnp.bfloat16

VMEM_LIMIT_BYTES = 56 * 1024 * 1024
SUBLANES = 8
LANES = 128
COL_TILE = 256
ROW_CHUNK = 256


def _tiles(m_rows):
    return dict(
        norm=min(256, m_rows),
        ffn_up=min(2048, m_rows),
        down=min(512, m_rows),
        out=min(512, m_rows),
        hyb_in=min(1024, m_rows),
        convb=min(256, m_rows),
        pool=min(256, m_rows),
    )


def _params(*sem):
    return pltpu.CompilerParams(dimension_semantics=sem, vmem_limit_bytes=VMEM_LIMIT_BYTES)


def _sigmoid(v):
    return 1.0 / (1.0 + jnp.exp(-v))


def _rms(x, g):
    ms = jnp.mean(x * x, axis=-1, keepdims=True)
    return x * lax.rsqrt(ms + EPS) * g


def _lane_ssq(x):
    x2 = x * x
    acc = x2[:, :LANES]
    for k in range(1, x.shape[1] // LANES):
        acc = acc + x2[:, k * LANES:(k + 1) * LANES]
    return acc


def _lane_total(p):
    return jnp.broadcast_to(jnp.sum(p, axis=-1, keepdims=True), p.shape)


def _inv_rms(ssq, d, width):
    inv = lax.rsqrt(ssq / d + EPS)
    return jnp.concatenate([inv] * (width // LANES), axis=-1)


def _side_cast_specs(src, layer, n_steps, step_of):
    _, rows, cols = src.shape
    slab = rows // n_steps
    assert slab * n_steps == rows and slab % (2 * SUBLANES) == 0
    in_spec = pl.BlockSpec((None, slab, cols), lambda *g: (layer, step_of(*g), 0))
    out_spec = pl.BlockSpec((slab, cols), lambda *g: (step_of(*g), 0))
    return in_spec, out_spec, jax.ShapeDtypeStruct((rows, cols), BF16)


def _side_cast(src_ref, dst_ref):
    dst_ref[...] = src_ref[...].astype(dst_ref.dtype)


def _rmsnorm_kernel(x_ref, g_ref, o_ref):
    o_ref[...] = _rms(x_ref[...], g_ref[...]).astype(o_ref.dtype)


def _rmsnorm(x, g, out_dtype, tm):
    m, d = x.shape
    return pl.pallas_call(
        _rmsnorm_kernel,
        grid=(m // tm,),
        in_specs=[pl.BlockSpec((tm, d), lambda i: (i, 0)),
                  pl.BlockSpec((1, d), lambda i: (0, 0))],
        out_specs=pl.BlockSpec((tm, d), lambda i: (i, 0)),
        out_shape=jax.ShapeDtypeStruct((m, d), out_dtype),
        compiler_params=_params("parallel"),
        name="rmsnorm",
    )(x, g.reshape(1, d))


def _norm_prep_kernel(x_ref, g_ref, xg_ref, ssq_ref):
    x = x_ref[...]
    xg_ref[...] = (x * g_ref[...]).astype(xg_ref.dtype)
    ssq_ref[...] = _lane_total(_lane_ssq(x))


def _norm_prep(x, g, tm):
    m, d = x.shape
    return pl.pallas_call(
        _norm_prep_kernel,
        grid=(m // tm,),
        in_specs=[pl.BlockSpec((tm, d), lambda i: (i, 0)),
                  pl.BlockSpec((1, d), lambda i: (0, 0))],
        out_specs=[pl.BlockSpec((tm, d), lambda i: (i, 0)),
                   pl.BlockSpec((tm, LANES), lambda i: (i, 0))],
        out_shape=[jax.ShapeDtypeStruct((m, d), BF16),
                   jax.ShapeDtypeStruct((m, LANES), F32)],
        compiler_params=_params("parallel"),
        name="norm_prep",
    )(x, g.reshape(1, d))


def _conv3_chunk(u, prev8, taps):
    rows = u.shape[0]
    ext = jnp.concatenate([prev8, u], axis=0)
    conv = (taps[2:3, :] * u
            + taps[1:2, :] * ext[SUBLANES - 1:SUBLANES - 1 + rows, :]
            + taps[0:1, :] * ext[SUBLANES - 2:SUBLANES - 2 + rows, :])
    return conv, u[rows - SUBLANES:, :]


def _halo_load(halo_ref, m, n):
    @pl.when(m == 0)
    def _():
        halo_ref[n] = jnp.zeros(halo_ref.shape[1:], F32)

    return halo_ref[n]


def _ffn_up_kernel(xg_ref, ssq_ref, wg_ref, wu_ref, cg_ref, cu_ref, bg_ref, bu_ref, src_ref,
                   o_ref, dst_ref, halo_g, halo_u, *, tm, rc):
    m = pl.program_id(0)
    n = pl.program_id(1)
    d = xg_ref.shape[1]
    _side_cast(src_ref, dst_ref)
    prev_g = _halo_load(halo_g, m, n)
    prev_u = _halo_load(halo_u, m, n)
    cg = cg_ref[...]
    cu = cu_ref[...]
    for c in range(tm // rc):
        r0 = c * rc
        xg = xg_ref[r0:r0 + rc, :]
        inv = _inv_rms(ssq_ref[r0:r0 + rc, :], d, o_ref.shape[1])
        ug = jnp.dot(xg, wg_ref[...], preferred_element_type=F32) * inv
        uu = jnp.dot(xg, wu_ref[...], preferred_element_type=F32) * inv
        gate, prev_g = _conv3_chunk(ug, prev_g, cg)
        up, prev_u = _conv3_chunk(uu, prev_u, cu)
        gate = gate + bg_ref[...]
        up = up + bu_ref[...]
        o_ref[r0:r0 + rc, :] = ((gate * _sigmoid(gate)) * up).astype(o_ref.dtype)
    halo_g[n] = prev_g
    halo_u[n] = prev_u


def _ffn_up(xg, ssq, w_up, conv, bias, cast_src, layer, tm):
    m, d = xg.shape
    f = w_up.shape[1] // 2
    tn = COL_TILE
    nt = f // tn
    bias2 = bias.reshape(1, 2 * f)
    kt = conv.shape[0]
    c_in, c_out, c_shape = _side_cast_specs(cast_src, layer, (m // tm) * nt, lambda i, j: i * nt + j)
    return pl.pallas_call(
        functools.partial(_ffn_up_kernel, tm=tm, rc=min(ROW_CHUNK, tm)),
        grid=(m // tm, nt),
        in_specs=[
            pl.BlockSpec((tm, d), lambda i, j: (i, 0)),
            pl.BlockSpec((tm, LANES), lambda i, j: (i, 0)),
            pl.BlockSpec((d, tn), lambda i, j: (0, j)),
            pl.BlockSpec((d, tn), lambda i, j: (0, nt + j)),
            pl.BlockSpec((kt, tn), lambda i, j: (0, j)),
            pl.BlockSpec((kt, tn), lambda i, j: (0, nt + j)),
            pl.BlockSpec((1, tn), lambda i, j: (0, j)),
            pl.BlockSpec((1, tn), lambda i, j: (0, nt + j)),
            c_in,
        ],
        out_specs=[pl.BlockSpec((tm, tn), lambda i, j: (i, j)), c_out],
        out_shape=[jax.ShapeDtypeStruct((m, f), BF16), c_shape],
        scratch_shapes=[
            pltpu.VMEM((nt, SUBLANES, tn), F32),
            pltpu.VMEM((nt, SUBLANES, tn), F32),
        ],
        compiler_params=_params("arbitrary", "arbitrary"),
        name="ffn_up",
    )(xg, ssq, w_up, w_up, conv, conv, bias2, bias2, cast_src)


def _mm_res_kernel(*refs, n_a, emit):
    a_refs, w_refs, x_ref = refs[:n_a], refs[n_a:2 * n_a], refs[2 * n_a]
    acc = None
    for a_ref, w_ref in zip(a_refs, w_refs):
        part = jnp.dot(a_ref[...], w_ref[...], preferred_element_type=F32)
        acc = part if acc is None else acc + part
    xn = x_ref[...] + acc
    if not emit:
        refs[-1][...] = xn
        return
    g_ref, o_ref, xg_ref, ssq_ref = refs[2 * n_a + 1:]
    o_ref[...] = xn
    xg_ref[...] = (xn * g_ref[...]).astype(xg_ref.dtype)
    j = pl.program_id(1)

    part = _lane_ssq(xn)

    @pl.when(j == 0)
    def _():
        ssq_ref[...] = part

    @pl.when(j > 0)
    def _():
        ssq_ref[...] += part

    @pl.when(j == pl.num_programs(1) - 1)
    def _():
        ssq_ref[...] = _lane_total(ssq_ref[...])


def _mm_res(a_list, w, layer, x, tm, tn, g_next=None):
    m = x.shape[0]
    n = w.shape[-1]
    ks = [a.shape[1] for a in a_list]
    assert len(set(ks)) == 1 and w.shape[-2] == sum(ks) and (layer is None) == (w.ndim == 2)
    k = ks[0]
    emit = g_next is not None
    in_specs = [pl.BlockSpec((tm, k), lambda i, j: (i, 0)) for _ in a_list]
    if layer is None:
        in_specs += [pl.BlockSpec((k, tn), lambda i, j, p=p: (p, j)) for p in range(len(a_list))]
    else:
        in_specs += [pl.BlockSpec((None, k, tn), lambda i, j, p=p: (layer, p, j)) for p in range(len(a_list))]
    in_specs += [pl.BlockSpec((tm, tn), lambda i, j: (i, j))]
    out_specs = [pl.BlockSpec((tm, tn), lambda i, j: (i, j))]
    out_shape = [jax.ShapeDtypeStruct((m, n), F32)]
    args = list(a_list) + [w] * len(a_list) + [x]
    if emit:
        in_specs += [pl.BlockSpec((1, tn), lambda i, j: (0, j))]
        args += [g_next.reshape(1, n)]
        out_specs += [pl.BlockSpec((tm, tn), lambda i, j: (i, j)),
                      pl.BlockSpec((tm, LANES), lambda i, j: (i, 0))]
        out_shape += [jax.ShapeDtypeStruct((m, n), BF16),
                      jax.ShapeDtypeStruct((m, LANES), F32)]
    outs = pl.pallas_call(
        functools.partial(_mm_res_kernel, n_a=len(a_list), emit=emit),
        grid=(m // tm, n // tn),
        in_specs=in_specs,
        out_specs=out_specs,
        out_shape=out_shape,
        compiler_params=_params("parallel", "arbitrary"),
        name="mm_res",
    )(*args)
    return outs if emit else outs[0]


def _hyb_in_kernel(xg_ref, ssq_ref, wx_ref, wc_ref, wb_ref, wv_ref, wg_ref, ca_ref, src_ref,
                   ya_ref, u_ref, dst_ref, halo, *, tm, rc):
    m = pl.program_id(0)
    n = pl.program_id(1)
    d = xg_ref.shape[1]
    _side_cast(src_ref, dst_ref)
    prev = _halo_load(halo, m, n)
    ca = ca_ref[...]
    for c in range(tm // rc):
        r0 = c * rc
        xg = xg_ref[r0:r0 + rc, :]
        inv = _inv_rms(ssq_ref[r0:r0 + rc, :], d, ya_ref.shape[1])
        dot = lambda w_ref: jnp.dot(xg, w_ref[...], preferred_element_type=F32) * inv
        z = dot(wc_ref) * dot(wx_ref)
        conv, prev = _conv3_chunk(z, prev, ca)
        ya_ref[r0:r0 + rc, :] = (dot(wb_ref) * conv).astype(ya_ref.dtype)
        u_ref[r0:r0 + rc, :] = dot(wv_ref) * _sigmoid(dot(wg_ref))
    halo[n] = prev


def _hyb_in(xg, ssq, w_in, layer, conv_a, d_a, d_b, cast_src, cast_layer, tm):
    m, d = xg.shape
    assert d_a == d_b and w_in.shape[2] == 3 * d_a + 2 * d_b
    tn = COL_TILE
    nt = d_a // tn
    kt = conv_a.shape[0]
    wspec = lambda off: pl.BlockSpec((None, d, tn), lambda i, j: (layer, 0, off * nt + j))
    c_in, c_out, c_shape = _side_cast_specs(cast_src, cast_layer, (m // tm) * nt, lambda i, j: i * nt + j)
    return pl.pallas_call(
        functools.partial(_hyb_in_kernel, tm=tm, rc=min(ROW_CHUNK, tm)),
        grid=(m // tm, nt),
        in_specs=[pl.BlockSpec((tm, d), lambda i, j: (i, 0)),
                  pl.BlockSpec((tm, LANES), lambda i, j: (i, 0)),
                  wspec(0), wspec(1), wspec(2), wspec(3), wspec(4),
                  pl.BlockSpec((kt, tn), lambda i, j: (0, j)),
                  c_in],
        out_specs=[pl.BlockSpec((tm, tn), lambda i, j: (i, j)),
                   pl.BlockSpec((tm, tn), lambda i, j: (i, j)),
                   c_out],
        out_shape=[jax.ShapeDtypeStruct((m, d_a), BF16),
                   jax.ShapeDtypeStruct((m, d_b), F32),
                   c_shape],
        scratch_shapes=[pltpu.VMEM((nt, SUBLANES, tn), F32)],
        compiler_params=_params("arbitrary", "arbitrary"),
        name="hyb_in",
    )(xg, ssq, w_in, w_in, w_in, w_in, w_in, conv_a, cast_src)


def _convb_kernel(u_ref, uh_ref, cw_ref, cb_ref, g_ref, b_ref, src_ref, o_ref, dst_ref, ext, cv,
                  *, tm, halo, kw):
    m = pl.program_id(0)
    c = u_ref.shape[1]
    _side_cast(src_ref, dst_ref)

    @pl.when(m == 0)
    def _():
        ext[0:halo, :] = jnp.zeros((halo, c), F32)

    @pl.when(m > 0)
    def _():
        ext[0:halo, :] = uh_ref[...]

    ext[halo:, :] = u_ref[...]
    base = halo - (kw - 1)

    def col_block(cb, carry):
        cols = pl.ds(pl.multiple_of(cb * LANES, LANES), LANES)
        acc = None
        for r in range(SUBLANES):
            rows = tm if r == 0 else tm + SUBLANES
            q = None
            for o in range(r, halo + 1, SUBLANES):
                j = o - base
                if 0 <= j < kw:
                    term = cw_ref[j:j + 1, cols] * ext[pl.ds(o - r, rows), cols]
                    q = term if q is None else q + term
            acc = q if r == 0 else acc + q[r:r + tm, :]
        cv[:, cols] = acc
        return carry

    lax.fori_loop(0, c // LANES, col_block, 0)
    v = cv[...] + cb_ref[...]
    mu = jnp.mean(v, axis=-1, keepdims=True)
    vc = v - mu
    y = vc * lax.rsqrt(jnp.mean(vc * vc, axis=-1, keepdims=True) + EPS)
    y = y * g_ref[...] + b_ref[...]
    o_ref[...] = (y * _sigmoid(y)).astype(o_ref.dtype)


def _convb(u, conv_w, conv_b, ln_g, ln_b, cast_src, cast_layer, tm):
    m, c = u.shape
    kw = conv_w.shape[0]
    halo = -(-(kw - 1) // SUBLANES) * SUBLANES
    assert tm % halo == 0
    r = tm // halo
    row = lambda v: v.reshape(1, c)
    c_in, c_out, c_shape = _side_cast_specs(cast_src, cast_layer, m // tm, lambda i: i)
    return pl.pallas_call(
        functools.partial(_convb_kernel, tm=tm, halo=halo, kw=kw),
        grid=(m // tm,),
        in_specs=[pl.BlockSpec((tm, c), lambda i: (i, 0)),
                  pl.BlockSpec((halo, c), lambda i: (jnp.maximum(i * r - 1, 0), 0)),
                  pl.BlockSpec((kw, c), lambda i: (0, 0)),
                  pl.BlockSpec((1, c), lambda i: (0, 0)),
                  pl.BlockSpec((1, c), lambda i: (0, 0)),
                  pl.BlockSpec((1, c), lambda i: (0, 0)),
                  c_in],
        out_specs=[pl.BlockSpec((tm, c), lambda i: (i, 0)), c_out],
        out_shape=[jax.ShapeDtypeStruct((m, c), BF16), c_shape],
        scratch_shapes=[pltpu.VMEM((tm + halo, c), F32),
                        pltpu.VMEM((tm, c), F32)],
        compiler_params=_params("parallel"),
        name="convb_ln",
    )(u, u, conv_w, row(conv_b), row(ln_g), row(ln_b), cast_src)


def _pool_kernel(x_ref, xh_ref, g_ref, w_ref, s_ref, gn_ref, o_ref, xg_ref, ssq_ref, he,
                 *, tm, halo, windows):
    m = pl.program_id(0)
    g = g_ref[...]
    d = x_ref.shape[1]
    gc = d // len(windows)

    @pl.when(m == 0)
    def _():
        he[0:halo, :] = jnp.zeros((halo, d), F32)

    @pl.when(m > 0)
    def _():
        he[0:halo, :] = _rms(xh_ref[...], g)

    he[halo:, :] = _rms(x_ref[...], g)
    pos = m * tm + lax.broadcasted_iota(jnp.int32, (tm, 1), 0)
    ssq = jnp.zeros(ssq_ref.shape, F32)
    for gi, win in enumerate(windows):
        cols = slice(gi * gc, (gi + 1) * gc)
        tok = he[pl.ds(halo, tm), cols]
        s = tok
        for k in range(1, win):
            s = s + he[pl.ds(halo - k, tm), cols]
        cnt = jnp.minimum(pos + 1, win).astype(F32)
        dp = s / cnt - tok
        y = jnp.dot(dp.astype(BF16), w_ref[gi], preferred_element_type=F32)
        xn = x_ref[:, cols] + y * s_ref[:, cols]
        o_ref[:, cols] = xn
        xg_ref[:, cols] = (xn * gn_ref[:, cols]).astype(xg_ref.dtype)
        ssq = ssq + _lane_ssq(xn)
    ssq_ref[...] = _lane_total(ssq)


def _pool_mixer(x, g, w_groups, layer, scale, g_next, tm):
    m, d = x.shape
    _, ng, gc, _ = w_groups.shape
    windows = POOL_WINDOWS
    assert ng == len(windows) and ng * gc == d
    halo = -(-(max(windows) - 1) // SUBLANES) * SUBLANES
    assert tm % halo == 0
    r = tm // halo
    return pl.pallas_call(
        functools.partial(_pool_kernel, tm=tm, halo=halo, windows=windows),
        grid=(m // tm,),
        in_specs=[pl.BlockSpec((tm, d), lambda i: (i, 0)),
                  pl.BlockSpec((halo, d), lambda i: (jnp.maximum(i * r - 1, 0), 0)),
                  pl.BlockSpec((1, d), lambda i: (0, 0)),
                  pl.BlockSpec((None, ng, gc, gc), lambda i: (layer, 0, 0, 0)),
                  pl.BlockSpec((1, d), lambda i: (0, 0)),
                  pl.BlockSpec((1, d), lambda i: (0, 0))],
        out_specs=[pl.BlockSpec((tm, d), lambda i: (i, 0)),
                   pl.BlockSpec((tm, d), lambda i: (i, 0)),
                   pl.BlockSpec((tm, LANES), lambda i: (i, 0))],
        out_shape=[jax.ShapeDtypeStruct((m, d), F32),
                   jax.ShapeDtypeStruct((m, d), BF16),
                   jax.ShapeDtypeStruct((m, LANES), F32)],
        scratch_shapes=[pltpu.VMEM((tm + halo, d), F32)],
        compiler_params=_params("parallel"),
        name="pool_mixer",
    )(x, x, g.reshape(1, d), w_groups, scale.reshape(1, d), g_next.reshape(1, d))


def kernel(x, mix_norm, ffn_norm, final_norm, hyb_w_in, hyb_conv_a, hyb_conv_b,
           hyb_conv_b_bias, hyb_ln_g, hyb_ln_b, hyb_w_out, pool_w, pool_scale,
           ffn_w_up, ffn_conv, ffn_conv_bias, ffn_w_down):
    b, s, d = x.shape
    assert b == 1, "causal halos are carried across row tiles of a single sequence"
    depth = mix_norm.shape[0]
    d_a = hyb_conv_a.shape[2]
    d_b = hyb_conv_b.shape[2]
    t = _tiles(s)
    xs = x.reshape(s, d)
    w_in = hyb_w_in.astype(BF16)
    w_out = hyb_w_out.astype(BF16)
    w_pool = pool_w.astype(BF16)
    assert depth % 2 == 0, "odd layers' FFN weights are cast inside the preceding even layer"

    xg, ssq = _norm_prep(xs, mix_norm[0], t["norm"])
    w_up_next = None
    for l in range(depth):
        i = l // 2
        if l % 2 == 0:
            ya, u, w_up = _hyb_in(xg, ssq, w_in, i, hyb_conv_a[i], d_a, d_b, ffn_w_up, l, t["hyb_in"])
            yb, w_up_next = _convb(u, hyb_conv_b[i], hyb_conv_b_bias[i], hyb_ln_g[i], hyb_ln_b[i],
                                   ffn_w_up, l + 1, t["convb"])
            xs, xg, ssq = _mm_res([ya, yb], w_out, i, xs, t["out"], 2 * COL_TILE, g_next=ffn_norm[l])
        else:
            w_up = w_up_next
            xs, xg, ssq = _pool_mixer(xs, mix_norm[l], w_pool, i, pool_scale[i], ffn_norm[l], t["pool"])
        act, w_down = _ffn_up(xg, ssq, w_up, ffn_conv[l], ffn_conv_bias[l], ffn_w_down, l, t["ffn_up"])
        if l + 1 < depth and (l + 1) % 2 == 0:
            xs, xg, ssq = _mm_res([act], w_down, None, xs, t["down"], 2 * COL_TILE, g_next=mix_norm[l + 1])
        else:
            xs = _mm_res([act], w_down, None, xs, t["down"], 2 * COL_TILE)
    out = _rmsnorm(xs, final_norm, x.dtype, t["norm"])
    return out.reshape(b, s, d)
```
